```python
import math
import jax, jax.numpy as jnp
from jax import lax
import numpy as np

D_MODEL = 1024
BATCH = 4
SEQ = 4096
DEPTH = 4
DEC_BATCH = 32
DEC_SEQ = 4
PAST_LEN = 8192
PAGE_SIZE = 128

HEAD_DIM = 64
N_HEADS = D_MODEL // HEAD_DIM
FOX_HEADS = (3 * N_HEADS) // 8
DIFF_HEADS = N_HEADS // 4
NSA_HEADS = N_HEADS - FOX_HEADS - DIFF_HEADS
DIFF_SUB = HEAD_DIM // 2
DIFF_KV_HEADS = 2
NSA_KV_HEADS = 2
DIFF_GROUP = DIFF_HEADS // DIFF_KV_HEADS
NSA_GROUP = NSA_HEADS // NSA_KV_HEADS
D_FF = 4 * D_MODEL
ROPE_THETA = 10000.0
Q_BLOCK = 128
NSA_BLOCK = 64
NSA_TOPK = 16
WINDOW = 512
FORCED_SCORE = 1.0e4
EPS = 1e-6
IN_SPLITS = (FOX_HEADS * HEAD_DIM, FOX_HEADS * HEAD_DIM, FOX_HEADS * HEAD_DIM, FOX_HEADS,
             DIFF_HEADS * 2 * DIFF_SUB, DIFF_KV_HEADS * 2 * DIFF_SUB, DIFF_KV_HEADS * HEAD_DIM,
             NSA_HEADS * HEAD_DIM, 6 * NSA_KV_HEADS * HEAD_DIM, 3 * NSA_HEADS)
IN_WIDTH = int(sum(IN_SPLITS))
SPLIT_POINTS = tuple(int(c) for c in np.cumsum(IN_SPLITS)[:-1])

kernel_name = 'hybrid_fox_diff_nsa_decode_step'


def rms_norm(x, g):
    xf = x.astype(jnp.float32)
    y = xf * lax.rsqrt(jnp.mean(xf * xf, axis=-1, keepdims=True) + EPS)
    return (y * g.astype(jnp.float32)).astype(x.dtype)


def rope(x, pos):
    d = x.shape[-1]
    inv = jnp.exp(-math.log(ROPE_THETA) * jnp.arange(0, d, 2, dtype=jnp.float32) / d)
    ang = pos.astype(jnp.float32)[:, None] * inv[None, :]
    cos = jnp.cos(ang)[None, :, None, :].astype(x.dtype)
    sin = jnp.sin(ang)[None, :, None, :].astype(x.dtype)
    x1, x2 = x[..., : d // 2], x[..., d // 2:]
    return jnp.concatenate([x1 * cos - x2 * sin, x2 * cos + x1 * sin], axis=-1)


def masked_softmax(logits, mask):
    lg = jnp.where(mask, logits.astype(jnp.float32), -jnp.inf)
    m = jnp.max(lg, axis=-1, keepdims=True)
    m = jnp.where(jnp.isfinite(m), m, 0.0)
    e = jnp.exp(lg - m)
    return e / jnp.maximum(jnp.sum(e, axis=-1, keepdims=True), 1e-30)


def project(u, l, pos, p):
    B, T, _ = u.shape
    z = jnp.einsum('btd,dp->btp', u, p['w_in'][l])
    fq, fk, fv, ff, dq, dk, dv, nq, nkv, ng = jnp.split(z, SPLIT_POINTS, axis=-1)
    fq = rms_norm(fq.reshape(B, T, FOX_HEADS, HEAD_DIM), p['fox_qn'][l])
    fk = rms_norm(fk.reshape(B, T, FOX_HEADS, HEAD_DIM), p['fox_kn'][l])
    fv = fv.reshape(B, T, FOX_HEADS, HEAD_DIM)
    flogf = jax.nn.log_sigmoid((ff + p['b_fox_f'][l]).astype(jnp.float32))
    dq = rope(rms_norm(dq.reshape(B, T, DIFF_HEADS * 2, DIFF_SUB), p['diff_qn'][l]), pos)
    dq = dq.reshape(B, T, DIFF_HEADS, 2, DIFF_SUB)
    dk = rope(rms_norm(dk.reshape(B, T, DIFF_KV_HEADS * 2, DIFF_SUB), p['diff_kn'][l]), pos)
    dk = dk.reshape(B, T, DIFF_KV_HEADS, 2, DIFF_SUB)
    dv = dv.reshape(B, T, DIFF_KV_HEADS, HEAD_DIM)
    nq = rope(rms_norm(nq.reshape(B, T, NSA_HEADS, HEAD_DIM), p['nsa_qn'][l]), pos)
    nkv = nkv.reshape(B, T, 6, NSA_KV_HEADS, HEAD_DIM)
    kn = p['nsa_kn'][l]
    ck = rope(rms_norm(nkv[:, :, 0], kn[0]), pos)
    cv = nkv[:, :, 1]
    sk = rope(rms_norm(nkv[:, :, 2], kn[1]), pos)
    sv = nkv[:, :, 3]
    wk = rope(rms_norm(nkv[:, :, 4], kn[2]), pos)
    wv = nkv[:, :, 5]
    gates = jax.nn.sigmoid(ng + p['nsa_gate_b'][l]).reshape(B, T, 3, NSA_HEADS)
    return (fq, fk, fv, flogf, dq, dk, dv, nq, ck, cv, sk, sv, wk, wv, gates)


def fox_attend(q, qc, k, v, kc, q_pos, k_pos):
    s = jnp.einsum('bqhd,bkhd->bhqk', q, k).astype(jnp.float32) * (HEAD_DIM ** -0.5)
    s = s + jnp.swapaxes(qc, 1, 2)[..., None] - jnp.swapaxes(kc, 1, 2)[:, :, None, :]
    pr = masked_softmax(s, k_pos[None, :] <= q_pos[:, None])
    return jnp.einsum('bhqk,bkhd->bqhd', pr.astype(v.dtype), v)


def diff_lambda_value(lp, l):
    lp = lp.astype(jnp.float32)
    lam_init = 0.8 - 0.6 * math.exp(-0.3 * l)
    lam = jnp.exp(jnp.sum(lp[0] * lp[1])) - jnp.exp(jnp.sum(lp[2] * lp[3])) + lam_init
    return lam, lam_init


def diff_attend(q, k, v, q_pos, k_pos, lam):
    B, Q = q.shape[:2]
    qg = q.reshape(B, Q, DIFF_KV_HEADS, DIFF_GROUP, 2, DIFF_SUB)
    s = jnp.einsum('bqgrcd,bkgcd->bgrcqk', qg, k).astype(jnp.float32) * (DIFF_SUB ** -0.5)
    pr = masked_softmax(s, k_pos[None, :] <= q_pos[:, None])
    a = pr[:, :, :, 0] - lam * pr[:, :, :, 1]
    o = jnp.einsum('bgrqk,bkgd->bqgrd', a.astype(v.dtype), v)
    return o.reshape(B, Q, DIFF_HEADS, HEAD_DIM)


def nsa_compress(k, v, wpos):
    B, Tk = k.shape[:2]
    nb = -(-Tk // NSA_BLOCK)
    pad = ((0, 0), (0, nb * NSA_BLOCK - Tk), (0, 0), (0, 0))
    kb = jnp.pad(k, pad).reshape(B, nb, NSA_BLOCK, NSA_KV_HEADS, HEAD_DIM)
    vb = jnp.pad(v, pad).reshape(B, nb, NSA_BLOCK, NSA_KV_HEADS, HEAD_DIM)
    ck = jnp.mean(kb * wpos[0][None, None, :, None, :], axis=2)
    cv = jnp.mean(vb * wpos[1][None, None, :, None, :], axis=2)
    return ck, cv


def nsa_blocks(a):
    B, Tk = a.shape[:2]
    nb = -(-Tk // NSA_BLOCK)
    a = jnp.pad(a, ((0, 0), (0, nb * NSA_BLOCK - Tk), (0, 0), (0, 0)))
    a = a.reshape(B, nb, NSA_BLOCK, NSA_KV_HEADS, HEAD_DIM)
    return jnp.transpose(a, (0, 3, 1, 2, 4))


_gather_blocks = jax.vmap(jax.vmap(lambda blocks, ix: blocks[ix]))


def nsa_attend(q, gates, q_pos, cmp_k, cmp_v, sel_kb, sel_vb, win_k, win_v, win_pos):
    B, Q = q.shape[:2]
    nb = cmp_k.shape[1]
    n_sel = min(NSA_TOPK, nb)
    scale = HEAD_DIM ** -0.5
    qg = q.reshape(B, Q, NSA_KV_HEADS, NSA_GROUP, HEAD_DIM)
    blk = jnp.arange(nb, dtype=jnp.int32)
    sc = jnp.einsum('bqgrd,bngd->bgrqn', qg, cmp_k).astype(jnp.float32) * scale
    pc = masked_softmax(sc, (blk[None, :] + 1) * NSA_BLOCK - 1 <= q_pos[:, None])
    o_cmp = jnp.einsum('bgrqn,bngd->bqgrd', pc.astype(q.dtype), cmp_v)
    imp = jnp.sum(pc, axis=2)
    cur = q_pos // NSA_BLOCK
    forced = (blk[None, :] == 0) | (blk[None, :] == cur[:, None]) | (blk[None, :] == cur[:, None] - 1)
    imp = jnp.where(forced, FORCED_SCORE, imp)
    imp = jnp.where(blk[None, :] <= cur[:, None], imp, -1.0)
    _, idx = lax.top_k(imp, n_sel)
    flat = idx.reshape(B, NSA_KV_HEADS, Q * n_sel)
    ksel = _gather_blocks(sel_kb, flat).reshape(B, NSA_KV_HEADS, Q, n_sel * NSA_BLOCK, HEAD_DIM)
    vsel = _gather_blocks(sel_vb, flat).reshape(B, NSA_KV_HEADS, Q, n_sel * NSA_BLOCK, HEAD_DIM)
    kpos = (idx[..., None] * NSA_BLOCK + jnp.arange(NSA_BLOCK, dtype=jnp.int32)).reshape(B, NSA_KV_HEADS, Q, n_sel * NSA_BLOCK)
    ss = jnp.einsum('bqgrd,bgqkd->bgrqk', qg, ksel).astype(jnp.float32) * scale
    ps = masked_softmax(ss, (kpos <= q_pos[None, None, :, None])[:, :, None])
    o_sel = jnp.einsum('bgrqk,bgqkd->bqgrd', ps.astype(q.dtype), vsel)
    sw = jnp.einsum('bqgrd,bkgd->bgrqk', qg, win_k).astype(jnp.float32) * scale
    dist = q_pos[:, None] - win_pos[None, :]
    pw = masked_softmax(sw, (dist >= 0) & (dist <= WINDOW) & (win_pos[None, :] >= 0))
    o_win = jnp.einsum('bgrqk,bkgd->bqgrd', pw.astype(q.dtype), win_v)
    shp = (B, Q, NSA_HEADS, HEAD_DIM)
    return (gates[:, :, 0, :, None] * o_cmp.reshape(shp) + gates[:, :, 1, :, None] * o_sel.reshape(shp)
            + gates[:, :, 2, :, None] * o_win.reshape(shp))


def merge_heads(o_a, o_b, o_c, l, lam_init, p):
    B, T = o_a.shape[:2]
    o_b = rms_norm(o_b, p['diff_subln'][l]) * (1.0 - lam_init)
    mix = jnp.concatenate([o_a.reshape(B, T, -1), o_b.reshape(B, T, -1), o_c.reshape(B, T, -1)], axis=-1)
    return jnp.einsum('btm,md->btd', mix, p['w_o'][l])


def sq_relu_mlp(h, w1, w2):
    a = jax.nn.relu(jnp.einsum('btd,df->btf', h, w1))
    return jnp.einsum('btf,fd->btd', a * a, w2)


def prompt_mixers(u, l, p):
    B, T, _ = u.shape
    pos = jnp.arange(T, dtype=jnp.int32)
    (fq, fk, fv, flogf, dq, dk, dv, nq, ck, cv, sk, sv, wk, wv, gates) = project(u, l, pos, p)
    fc = jnp.cumsum(flogf, axis=1)
    lam, lam_init = diff_lambda_value(p['diff_lambda'][l], l)
    cmp_k, cmp_v = nsa_compress(ck, cv, p['nsa_cmp_w'][l])
    sel_kb, sel_vb = nsa_blocks(sk), nsa_blocks(sv)
    wpad = ((0, 0), (WINDOW, 0), (0, 0), (0, 0))
    wk_pad, wv_pad = jnp.pad(wk, wpad), jnp.pad(wv, wpad)
    n_qb = T // Q_BLOCK
    wlen = WINDOW + Q_BLOCK

    def blockify(a):
        return jnp.swapaxes(a.reshape((B, n_qb, Q_BLOCK) + a.shape[2:]), 0, 1)

    def unblock(a):
        return jnp.swapaxes(a, 0, 1).reshape((B, T) + a.shape[3:])

    def one_block(args):
        i, fq_b, fc_b, dq_b, nq_b, g_b = args
        start = i * Q_BLOCK
        qpos = start + jnp.arange(Q_BLOCK, dtype=jnp.int32)
        o_a = fox_attend(fq_b, fc_b, fk, fv, fc, qpos, pos)
        o_b = diff_attend(dq_b, dk, dv, qpos, pos, lam)
        wkb = lax.dynamic_slice_in_dim(wk_pad, start, wlen, axis=1)
        wvb = lax.dynamic_slice_in_dim(wv_pad, start, wlen, axis=1)
        wpos = start - WINDOW + jnp.arange(wlen, dtype=jnp.int32)
        o_c = nsa_attend(nq_b, g_b, qpos, cmp_k, cmp_v, sel_kb, sel_vb, wkb, wvb, wpos)
        return o_a, o_b, o_c

    o_a, o_b, o_c = lax.map(one_block, (jnp.arange(n_qb, dtype=jnp.int32), blockify(fq), blockify(fc),
                                         blockify(dq), blockify(nq), blockify(gates)))
    out = merge_heads(unblock(o_a), unblock(o_b), unblock(o_c), l, lam_init, p)
    win_buf = min(WINDOW, PAST_LEN)
    last = lambda a: jnp.pad(a, ((0, 0), (win_buf, 0), (0, 0), (0, 0)))[:, -win_buf:]
    return out, (fk, fv, flogf, dk, dv, ck, cv, sk, sv, last(wk), last(wv))


def sample_mixers(u, l, p, cache, page_table):
    B, T, _ = u.shape
    past = page_table.shape[1] * PAGE_SIZE
    pos = past + jnp.arange(T, dtype=jnp.int32)
    kpos = jnp.arange(past + T, dtype=jnp.int32)
    (fq, fk, fv, flogf, dq, dk, dv, nq, ck, cv, sk, sv, wk, wv, gates) = project(u, l, pos, p)
    (c_fk, c_fv, c_fl, c_dk, c_dv, c_ck, c_cv, c_sk, c_sv, s_wk, s_wv) = cache

    def with_past(pool, new):
        g = pool[page_table]
        g = g.reshape((B, past) + pool.shape[2:]).astype(new.dtype)
        return jnp.concatenate([g, new], axis=1)

    fc_all = jnp.cumsum(with_past(c_fl, flogf), axis=1)
    o_a = fox_attend(fq, fc_all[:, past:], with_past(c_fk, fk), with_past(c_fv, fv), fc_all, pos, kpos)
    lam, lam_init = diff_lambda_value(p['diff_lambda'][l], l)
    o_b = diff_attend(dq, with_past(c_dk, dk), with_past(c_dv, dv), pos, kpos, lam)
    cmp_k, cmp_v = nsa_compress(with_past(c_ck, ck), with_past(c_cv, cv), p['nsa_cmp_w'][l])
    sel_kb, sel_vb = nsa_blocks(with_past(c_sk, sk)), nsa_blocks(with_past(c_sv, sv))
    win_buf = s_wk.shape[1]
    wk_all = jnp.concatenate([s_wk.astype(wk.dtype), wk], axis=1)
    wv_all = jnp.concatenate([s_wv.astype(wv.dtype), wv], axis=1)
    wpos = past - win_buf + jnp.arange(win_buf + T, dtype=jnp.int32)
    o_c = nsa_attend(nq, gates, pos, cmp_k, cmp_v, sel_kb, sel_vb, wk_all, wv_all, wpos)
    out = merge_heads(o_a, o_b, o_c, l, lam_init, p)
    return out, (fk, fv, flogf, dk, dv, ck, cv, sk, sv, wk_all[:, -win_buf:], wv_all[:, -win_buf:])


def setup_inputs(seed: int = 0) -> dict:
    key = jax.random.key(seed)
    ks = iter(jax.random.split(key, 40))
    f32 = jnp.float32
    n_pages = PAST_LEN // PAGE_SIZE
    n_used = DEC_BATCH * n_pages
    n_pool = n_used + n_used // 4
    win_buf = min(WINDOW, PAST_LEN)

    def nrm(shape, scale=1.0):
        return scale * jax.random.normal(next(ks), shape, f32)

    def gain(shape, s=0.02):
        return 1.0 + s * jax.random.normal(next(ks), shape, f32)

    b_fox_f = 2.0 + 5.0 * jax.random.uniform(next(ks), (DEPTH, FOX_HEADS), f32)
    rows = (DEPTH, n_pool, PAGE_SIZE)
    kv_nsa = (NSA_KV_HEADS, HEAD_DIM)
    perm = jax.random.permutation(next(ks), n_pool)[:n_used]
    return {
        'x_prompt': nrm((BATCH, SEQ, D_MODEL)),
        'x_sample': nrm((DEC_BATCH, DEC_SEQ, D_MODEL)),
        'cache_fox_k': nrm(rows + (FOX_HEADS, HEAD_DIM)),
        'cache_fox_v': nrm(rows + (FOX_HEADS, HEAD_DIM)),
        'cache_fox_logf': jax.nn.log_sigmoid(b_fox_f[:, None, None, :] + nrm(rows + (FOX_HEADS,))),
        'cache_diff_k': nrm(rows + (DIFF_KV_HEADS, 2, DIFF_SUB)),
        'cache_diff_v': nrm(rows + (DIFF_KV_HEADS, HEAD_DIM)),
        'cache_nsa_cmp_k': nrm(rows + kv_nsa),
        'cache_nsa_cmp_v': nrm(rows + kv_nsa),
        'cache_nsa_sel_k': nrm(rows + kv_nsa),
        'cache_nsa_sel_v': nrm(rows + kv_nsa),
        'state_nsa_win_k': nrm((DEPTH, DEC_BATCH, win_buf) + kv_nsa),
        'state_nsa_win_v': nrm((DEPTH, DEC_BATCH, win_buf) + kv_nsa),
        'page_table': perm.reshape(DEC_BATCH, n_pages).astype(jnp.int32),
        'w_in': nrm((DEPTH, D_MODEL, IN_WIDTH), D_MODEL ** -0.5),
        'b_fox_f': b_fox_f,
        'fox_qn': gain((DEPTH, HEAD_DIM)),
        'fox_kn': gain((DEPTH, HEAD_DIM)),
        'diff_qn': gain((DEPTH, DIFF_SUB)),
        'diff_kn': gain((DEPTH, DIFF_SUB)),
        'diff_lambda': nrm((DEPTH, 4, DIFF_SUB), 0.1),
        'diff_subln': gain((DEPTH, HEAD_DIM)),
        'nsa_qn': gain((DEPTH, HEAD_DIM)),
        'nsa_kn': gain((DEPTH, 3, HEAD_DIM)),
        'nsa_cmp_w': gain((DEPTH, 2, NSA_BLOCK, HEAD_DIM), 0.1),
        'nsa_gate_b': nrm((DEPTH, 3 * NSA_HEADS), 0.1),
        'w_o': nrm((DEPTH, D_MODEL, D_MODEL), D_MODEL ** -0.5),
        'norm1': gain((DEPTH, D_MODEL)),
        'norm2': gain((DEPTH, D_MODEL)),
        'w_ff1': nrm((DEPTH, D_MODEL, D_FF), D_MODEL ** -0.5),
        'w_ff2': nrm((DEPTH, D_FF, D_MODEL), D_FF ** -0.5),
    }


def reference(x_prompt, x_sample, cache_fox_k, cache_fox_v, cache_fox_logf, cache_diff_k, cache_diff_v,
              cache_nsa_cmp_k, cache_nsa_cmp_v, cache_nsa_sel_k, cache_nsa_sel_v, state_nsa_win_k,
              state_nsa_win_v, page_table, w_in, b_fox_f, fox_qn, fox_kn, diff_qn, diff_kn, diff_lambda,
              diff_subln, nsa_qn, nsa_kn, nsa_cmp_w, nsa_gate_b, w_o, norm1, norm2, w_ff1, w_ff2):
    p = dict(w_in=w_in, b_fox_f=b_fox_f, fox_qn=fox_qn, fox_kn=fox_kn, diff_qn=diff_qn, diff_kn=diff_kn,
             diff_lambda=diff_lambda, diff_subln=diff_subln, nsa_qn=nsa_qn, nsa_kn=nsa_kn,
             nsa_cmp_w=nsa_cmp_w, nsa_gate_b=nsa_gate_b, w_o=w_o)
    caches = (cache_fox_k, cache_fox_v, cache_fox_logf, cache_diff_k, cache_diff_v, cache_nsa_cmp_k,
              cache_nsa_cmp_v, cache_nsa_sel_k, cache_nsa_sel_v, state_nsa_win_k, state_nsa_win_v)
    n_state = len(caches)
    new_p = [[] for _ in range(n_state)]
    new_s = [[] for _ in range(n_state)]
    xp, xs = x_prompt, x_sample
    for l in range(DEPTH):
        mix_p, st_p = prompt_mixers(rms_norm(xp, norm1[l]), l, p)
        xp = xp + mix_p
        xp = xp + sq_relu_mlp(rms_norm(xp, norm2[l]), w_ff1[l], w_ff2[l])
        mix_s, st_s = sample_mixers(rms_norm(xs, norm1[l]), l, p, tuple(c[l] for c in caches), page_table)
        xs = xs + mix_s
        xs = xs + sq_relu_mlp(rms_norm(xs, norm2[l]), w_ff1[l], w_ff2[l])
        for j in range(n_state):
            new_p[j].append(st_p[j])
            new_s[j].append(st_s[j])
    (fk_p, fv_p, fl_p, dk_p, dv_p, ck_p, cv_p, sk_p, sv_p, wk_p, wv_p) = [jnp.stack(a, axis=0) for a in new_p]
    (fk_s, fv_s, fl_s, dk_s, dv_s, ck_s, cv_s, sk_s, sv_s, wk_s, wv_s) = [jnp.stack(a, axis=0) for a in new_s]
    return (xp, xs, fk_p, fk_s, fv_p, fv_s, fl_p, fl_s, dk_p, dk_s, dv_p, dv_s,
            ck_p, ck_s, cv_p, cv_s, sk_p, sk_s, sv_p, sv_s, wk_p, wk_s, wv_p, wv_s)
```

```python
import functools
import math

import jax
import jax.numpy as jnp
from jax import lax
from jax.experimental import pallas as pl
from jax.experimental.pallas import tpu as pltpu

F32 = jnp.float32
BF16 = jnp.bfloat16
I32 = jnp.int32

D_MODEL = 1024
HEAD_DIM = 64
FOX_HEADS = 6
DIFF_HEADS = 4
NSA_HEADS = 6
DIFF_SUB = 32
D_FF = 4 * D_MODEL
ROPE_THETA = 10000.0
NSA_BLOCK = 64
NSA_TOPK = 16
WINDOW = 512
PAGE_SIZE = 128
FORCED_SCORE = 1.0e4
EPS = 1e-6
NEG = -1.0e30
LANES = 128

C_FQ, C_FK, C_FV = 0, 384, 768
C_DQ, C_DK, C_DV = 1152, 1408, 1536
C_NQ, C_NKV, C_SMALL = 1664, 2048, 2816
W_PACKED = 2944
SPLITS = (384, 384, 384, 6, 256, 128, 128, 384, 768, 18)
SPLIT_POINTS = tuple(int(sum(SPLITS[:i + 1])) for i in range(len(SPLITS) - 1))

VMEM_LIMIT = 56 * 1024 * 1024


def _params(sem):
    return pltpu.CompilerParams(dimension_semantics=sem, vmem_limit_bytes=VMEM_LIMIT)


def _lane_iota(shape):
    return lax.broadcasted_iota(I32, shape, len(shape) - 1)


def _nt_dot(a, b):
    return lax.dot_general(a, b, (((1,), (1,)), ((), ())), preferred_element_type=F32)


def _dot(a, b):
    return jnp.dot(a, b, preferred_element_type=F32)


def _blockdiag(seg_log2):
    r = lax.broadcasted_iota(I32, (LANES, LANES), 0) >> seg_log2
    c = lax.broadcasted_iota(I32, (LANES, LANES), 1) >> seg_log2
    return jnp.where(r == c, 1.0 / (1 << seg_log2), 0.0).astype(BF16)


def _seg_mean(x2, bd):
    hi = x2.astype(BF16)
    lo = (x2 - hi.astype(F32)).astype(BF16)
    return _dot(hi, bd) + _dot(lo, bd)


def _seg_norm(x, gain, bd):
    return x * lax.rsqrt(_seg_mean(x * x, bd) + EPS) * gain


def _rope(x, cos, sin, half):
    lane = _lane_iota(x.shape)
    first = (lane & (2 * half - 1)) < half
    swapped = jnp.where(first, pltpu.roll(x, LANES - half, 1), pltpu.roll(x, half, 1))
    return x * cos + swapped * sin


def _softmax_step(s, mask, m, l):
    if mask is not None:
        s = jnp.where(mask, s, NEG)
    m_new = jnp.maximum(m, jnp.max(s, axis=1, keepdims=True))
    alpha = jnp.exp(m - m_new)
    p = jnp.exp(s - m_new)
    if mask is not None:
        p = jnp.where(mask, p, 0.0)
    l_new = alpha * l + jnp.sum(p, axis=1, keepdims=True)
    return p, alpha, m_new, l_new


def _flash_init(rows):
    return (jnp.full((rows, 1), NEG, F32), jnp.zeros((rows, 1), F32), jnp.zeros((rows, LANES), F32))


def _normalise(acc, l):
    return acc / jnp.maximum(l, 1e-30)


def _masked_softmax(s, vis, axis):
    lg = jnp.where(vis, s, NEG)
    mx = jnp.max(lg, axis=axis, keepdims=True)
    e = jnp.where(vis, jnp.exp(lg - mx), 0.0)
    return e / jnp.maximum(jnp.sum(e, axis=axis, keepdims=True), 1e-30)


def _topk_mask(imp, idx, k, axis):
    sel = jnp.zeros_like(imp)
    for _ in range(k):
        mx = jnp.max(imp, axis=axis, keepdims=True)
        first = jnp.min(jnp.where(imp == mx, idx, 1.0e9), axis=axis, keepdims=True)
        hit = idx == first
        sel = jnp.where(hit, 1.0, sel)
        imp = jnp.where(hit, -2.0, imp)
    return sel


def _proj_kernel(x_ref, g1_ref, w_ref, bias_ref, gains_ref, c64_ref, s64_ref, c32_ref, s32_ref,
                 wck_ref, wcv_ref,
                 fq_b, fk_f, fk_b, fv_f, fv_b, dq_b, dk_f, dk_b, dv_f, dv_b, nq_b,
                 ck_f, cv_f, sk_f, sv_f, wk_f, wv_f, sk_b, sv_b, wk_b, wv_b,
                 logf_o, gate_o, cmpk_o, cmpv_o):
    x = x_ref[...]
    u = x * lax.rsqrt(jnp.mean(x * x, axis=-1, keepdims=True) + EPS) * g1_ref[...]
    z = _dot(u.astype(BF16), w_ref[...])
    tm = x.shape[0]
    bd64 = _blockdiag(6)
    bd32 = _blockdiag(5)
    gains = gains_ref[...]
    c64, s64 = c64_ref[...], s64_ref[...]
    c32, s32 = c32_ref[...], s32_ref[...]
    scale = HEAD_DIM ** -0.5

    def chunk(col):
        return z[:, col:col + LANES]

    for c in range(3):
        sl = slice(c * LANES, (c + 1) * LANES)
        fq_b[:, sl] = (_seg_norm(chunk(C_FQ + c * LANES), gains[0:1], bd64) * scale).astype(BF16)
        fk = _seg_norm(chunk(C_FK + c * LANES), gains[1:2], bd64)
        fk_f[:, sl] = fk
        fk_b[:, sl] = fk.astype(BF16)
        fv = chunk(C_FV + c * LANES)
        fv_f[:, sl] = fv
        fv_b[:, sl] = fv.astype(BF16)
        nq = _rope(_seg_norm(chunk(C_NQ + c * LANES), gains[4:5], bd64), c64, s64, 32)
        nq_b[:, sl] = (nq * scale).astype(BF16)
    for c in range(2):
        sl = slice(c * LANES, (c + 1) * LANES)
        dq = _rope(_seg_norm(chunk(C_DQ + c * LANES), gains[2:3], bd32), c32, s32, 16)
        dq_b[:, sl] = dq.astype(BF16)
    dk = _rope(_seg_norm(chunk(C_DK), gains[3:4], bd32), c32, s32, 16)
    dk_f[...] = dk
    dk_b[...] = dk.astype(BF16)
    dv = chunk(C_DV)
    dv_f[...] = dv
    dv_b[...] = dv.astype(BF16)

    ck = _rope(_seg_norm(chunk(C_NKV), gains[5:6], bd64), c64, s64, 32)
    ck_f[...] = ck
    cv = chunk(C_NKV + LANES)
    cv_f[...] = cv
    nblk = tm // NSA_BLOCK
    cmpk_o[...] = jnp.sum((ck * wck_ref[...]).reshape(nblk, NSA_BLOCK, LANES), axis=1) * (1.0 / NSA_BLOCK)
    cmpv_o[...] = jnp.sum((cv * wcv_ref[...]).reshape(nblk, NSA_BLOCK, LANES), axis=1) * (1.0 / NSA_BLOCK)
    sk = _rope(_seg_norm(chunk(C_NKV + 2 * LANES), gains[6:7], bd64), c64, s64, 32)
    sk_f[...] = sk
    sk_b[...] = sk.astype(BF16)
    sv = chunk(C_NKV + 3 * LANES)
    sv_f[...] = sv
    sv_b[...] = sv.astype(BF16)
    wk = _rope(_seg_norm(chunk(C_NKV + 4 * LANES), gains[7:8], bd64), c64, s64, 32)
    wk_f[...] = wk
    wk_b[...] = wk.astype(BF16)
    wv = chunk(C_NKV + 5 * LANES)
    wv_f[...] = wv
    wv_b[...] = wv.astype(BF16)

    small = chunk(C_SMALL) + bias_ref[...]
    en = jnp.exp(-jnp.abs(small))
    logf_o[...] = jnp.minimum(small, 0.0) - jnp.log(1.0 + en)
    gate_o[...] = 1.0 / (1.0 + jnp.exp(-small))


_PROJ_OUT = (
    ("fq_b", 384, BF16), ("fk_f", 384, F32), ("fk_b", 384, BF16), ("fv_f", 384, F32), ("fv_b", 384, BF16),
    ("dq_b", 256, BF16), ("dk_f", 128, F32), ("dk_b", 128, BF16), ("dv_f", 128, F32), ("dv_b", 128, BF16),
    ("nq_b", 384, BF16),
    ("ck_f", 128, F32), ("cv_f", 128, F32), ("sk_f", 128, F32), ("sv_f", 128, F32), ("wk_f", 128, F32),
    ("wv_f", 128, F32), ("sk_b", 128, BF16), ("sv_b", 128, BF16), ("wk_b", 128, BF16), ("wv_b", 128, BF16),
    ("logf", 128, F32), ("gate", 128, F32),
)


def _project(x, lw, tabs, tm):
    n = x.shape[0]
    n_tab_blocks = tabs[0].shape[0] // tm
    nblk = tm // NSA_BLOCK
    const = lambda i: (0, 0)
    tok = lambda i: (i, 0)
    tab = lambda i: (i % n_tab_blocks, 0)
    in_specs = [
        pl.BlockSpec((tm, D_MODEL), tok),
        pl.BlockSpec((1, D_MODEL), const),
        pl.BlockSpec((D_MODEL, W_PACKED), const),
        pl.BlockSpec((1, LANES), const),
        pl.BlockSpec((8, LANES), const),
        pl.BlockSpec((tm, LANES), tab), pl.BlockSpec((tm, LANES), tab),
        pl.BlockSpec((tm, LANES), tab), pl.BlockSpec((tm, LANES), tab),
        pl.BlockSpec((tm, LANES), const), pl.BlockSpec((tm, LANES), const),
    ]
    out_shape = [jax.ShapeDtypeStruct((n, w), dt) for _, w, dt in _PROJ_OUT]
    out_specs = [pl.BlockSpec((tm, w), tok) for _, w, _ in _PROJ_OUT]
    for _ in range(2):
        out_shape.append(jax.ShapeDtypeStruct((n // tm, nblk, LANES), F32))
        out_specs.append(pl.BlockSpec((None, nblk, LANES), lambda i: (i, 0, 0)))
    outs = pl.pallas_call(
        _proj_kernel, grid=(n // tm,), in_specs=in_specs, out_specs=out_specs, out_shape=out_shape,
        compiler_params=_params(("parallel",)), name="proj",
    )(x, lw["g1"], lw["w_in"], lw["bias_small"], lw["gains"], *tabs, lw["wck"][:tm], lw["wcv"][:tm])
    res = {name: o for (name, _, _), o in zip(_PROJ_OUT, outs)}
    res["cmpk"] = outs[-2].reshape(n // NSA_BLOCK, LANES)
    res["cmpv"] = outs[-1].reshape(n // NSA_BLOCK, LANES)
    return res


def _post_kernel(x_ref, oa_ref, ob_ref, oc_ref, wo_ref, g2_ref, w1_ref, w2_ref, o_ref, *, ff_chunk):
    x = x_ref[...]
    mix = _dot(oa_ref[...].astype(BF16), wo_ref[0:384, :])
    mix = mix + _dot(ob_ref[...].astype(BF16), wo_ref[384:640, :])
    mix = mix + _dot(oc_ref[...].astype(BF16), wo_ref[640:1024, :])
    x1 = x + mix
    h = (x1 * lax.rsqrt(jnp.mean(x1 * x1, axis=-1, keepdims=True) + EPS) * g2_ref[...]).astype(BF16)
    acc = x1
    for c in range(D_FF // ff_chunk):
        a = jnp.maximum(_dot(h, w1_ref[:, c * ff_chunk:(c + 1) * ff_chunk]), 0.0)
        acc = acc + _dot((a * a).astype(BF16), w2_ref[c * ff_chunk:(c + 1) * ff_chunk, :])
    o_ref[...] = acc


def _post(x, oa, ob, oc, lw, tm):
    n = x.shape[0]
    tok = lambda i: (i, 0)
    const = lambda i: (0, 0)
    single = pl.Buffered(1)
    return pl.pallas_call(
        functools.partial(_post_kernel, ff_chunk=1024),
        grid=(n // tm,),
        in_specs=[
            pl.BlockSpec((tm, D_MODEL), tok),
            pl.BlockSpec((tm, 384), tok), pl.BlockSpec((tm, 256), tok), pl.BlockSpec((tm, 384), tok),
            pl.BlockSpec((D_MODEL, D_MODEL), const, pipeline_mode=single),
            pl.BlockSpec((1, D_MODEL), const),
            pl.BlockSpec((D_MODEL, D_FF), const, pipeline_mode=single),
            pl.BlockSpec((D_FF, D_MODEL), const, pipeline_mode=single),
        ],
        out_specs=pl.BlockSpec((tm, D_MODEL), tok),
        out_shape=jax.ShapeDtypeStruct((n, D_MODEL), F32),
        compiler_params=_params(("parallel",)), name="post",
    )(x, oa, ob, oc, lw["w_o"], lw["g2"], lw["w_ff1"], lw["w_ff2"])


def _cumsum_kernel(x_ref, o_ref, carry_ref, *, tc):
    @pl.when(pl.program_id(1) == 0)
    def _():
        carry_ref[...] = jnp.zeros_like(carry_ref)

    x = x_ref[...]
    tri = jnp.where(lax.broadcasted_iota(I32, (tc, tc), 0) <= lax.broadcasted_iota(I32, (tc, tc), 1),
                    1.0, 0.0).astype(BF16)
    hi = x.astype(BF16)
    r1 = x - hi.astype(F32)
    mid = r1.astype(BF16)
    lo = (r1 - mid.astype(F32)).astype(BF16)
    c = _dot(hi, tri) + (_dot(mid, tri) + _dot(lo, tri)) + carry_ref[:, 0:1]
    o_ref[...] = c
    carry_ref[...] = jnp.broadcast_to(c[:, tc - 1:tc], carry_ref.shape)


def _cumsum_rows(x, rb, tc=512):
    r, t = x.shape
    return pl.pallas_call(
        functools.partial(_cumsum_kernel, tc=tc),
        grid=(r // rb, t // tc),
        in_specs=[pl.BlockSpec((rb, tc), lambda i, j: (i, j))],
        out_specs=pl.BlockSpec((rb, tc), lambda i, j: (i, j)),
        out_shape=jax.ShapeDtypeStruct((r, t), F32),
        scratch_shapes=[pltpu.VMEM((rb, LANES), F32)],
        compiler_params=_params(("parallel", "arbitrary")), name="cumsum",
    )(x)


def _causal_bounds(q0, tq, tk):
    return q0 // tk, (q0 + tq - 1) // tk + 1


def _fox_kernel(q_ref, k_ref, v_ref, qc_ref, kc_ref, o_ref, *, tq, tk):
    q0 = pl.program_id(2) * tq
    q2 = q_ref[...]
    lo = _lane_iota((tq, LANES)) < HEAD_DIM
    zero = jnp.zeros_like(q2)
    qs = (jnp.where(lo, q2, zero), jnp.where(lo, zero, q2))
    qc = qc_ref[...]
    qcs = (qc[:, 0:1], qc[:, 1:2])
    rows = q0 + lax.broadcasted_iota(I32, (tq, tk), 0)
    cols = lax.broadcasted_iota(I32, (tq, tk), 1)

    def step(j, carry, masked):
        off = pl.multiple_of(j * tk, tk)
        kb = k_ref[pl.ds(off, tk), :]
        vb = v_ref[pl.ds(off, tk), :]
        mask = (rows >= off + cols) if masked else None
        out = []
        for e in range(2):
            m, l, acc = carry[e]
            s = _nt_dot(qs[e], kb) + (qcs[e] - kc_ref[e:e + 1, pl.ds(off, tk)])
            p, alpha, m, l = _softmax_step(s, mask, m, l)
            acc = alpha * acc + _dot(p.astype(BF16), vb)
            out.append((m, l, acc))
        return tuple(out)

    nfull, nall = _causal_bounds(q0, tq, tk)
    carry = lax.fori_loop(0, nfull, lambda j, c: step(j, c, False), (_flash_init(tq), _flash_init(tq)))
    carry = lax.fori_loop(nfull, nall, lambda j, c: step(j, c, True), carry)
    (_, l0, a0), (_, l1, a1) = carry
    o_ref[...] = jnp.where(lo, _normalise(a0, l0), _normalise(a1, l1)).astype(BF16)


def _fox_prompt(fq, fk, fv, qc, kc, tq=256, tk=256):
    b, t, _ = fq.shape
    return pl.pallas_call(
        functools.partial(_fox_kernel, tq=tq, tk=tk),
        grid=(b, 3, t // tq),
        in_specs=[
            pl.BlockSpec((None, tq, LANES), lambda i, p, q: (i, q, p)),
            pl.BlockSpec((None, t, LANES), lambda i, p, q: (i, 0, p)),
            pl.BlockSpec((None, t, LANES), lambda i, p, q: (i, 0, p)),
            pl.BlockSpec((None, None, tq, 8), lambda i, p, q: (i, p, q, 0)),
            pl.BlockSpec((None, None, 8, t), lambda i, p, q: (i, p, 0, 0)),
        ],
        out_specs=pl.BlockSpec((None, tq, LANES), lambda i, p, q: (i, q, p)),
        out_shape=jax.ShapeDtypeStruct((b, t, 384), BF16),
        compiler_params=_params(("parallel", "parallel", "arbitrary")), name="fox_prompt",
    )(fq, fk, fv, qc, kc)


def _diff_lambda(par):
    lam_init = par[4:5, 0:1]
    lam = (jnp.exp(jnp.sum(par[0:1] * par[1:2], axis=1, keepdims=True))
           - jnp.exp(jnp.sum(par[2:3] * par[3:4], axis=1, keepdims=True)) + lam_init)
    return lam, lam_init


def _diff_queries(q_ref, rows):
    lane = _lane_iota((rows, LANES))
    qs = []
    for g in range(2):
        qg = q_ref[:, g * LANES:(g + 1) * LANES].astype(F32)
        qr = pltpu.roll(qg, HEAD_DIM, 1)
        for r in range(2):
            src = qg if r == g else qr
            for c in range(2):
                lo_l = g * HEAD_DIM + c * DIFF_SUB
                msk = (lane >= lo_l) & (lane < lo_l + DIFF_SUB)
                qs.append(jnp.where(msk, src, 0.0).astype(BF16))
    return qs


def _diff_finish(carry, par, rows):
    lane = _lane_iota((rows, LANES))
    lam, lam_init = _diff_lambda(par)
    bd = _blockdiag(6)
    chunks = []
    for g in range(2):
        heads = []
        for r in range(2):
            _, l0, a0 = carry[g * 4 + r * 2]
            _, l1, a1 = carry[g * 4 + r * 2 + 1]
            o = _normalise(a0, l0) - lam * _normalise(a1, l1)
            if r != g:
                o = pltpu.roll(o, HEAD_DIM, 1)
            heads.append(o)
        chunk = jnp.where(lane < HEAD_DIM, heads[0], heads[1])
        chunks.append(_seg_norm(chunk, par[5:6], bd) * (1.0 - lam_init))
    return chunks


def _diff_kernel(q_ref, k_ref, v_ref, par_ref, o_ref, *, tq, tk):
    q0 = pl.program_id(1) * tq
    qs = _diff_queries(q_ref, tq)
    scale = DIFF_SUB ** -0.5
    rows = q0 + lax.broadcasted_iota(I32, (tq, tk), 0)
    cols = lax.broadcasted_iota(I32, (tq, tk), 1)

    def step(j, carry, masked):
        off = pl.multiple_of(j * tk, tk)
        kb = k_ref[pl.ds(off, tk), :]
        vb = v_ref[pl.ds(off, tk), :]
        mask = (rows >= off + cols) if masked else None
        out = []
        for i in range(8):
            m, l, acc = carry[i]
            s = _nt_dot(qs[i], kb) * scale
            p, alpha, m, l = _softmax_step(s, mask, m, l)
            acc = alpha * acc + _dot(p.astype(BF16), vb)
            out.append((m, l, acc))
        return tuple(out)

    nfull, nall = _causal_bounds(q0, tq, tk)
    carry = lax.fori_loop(0, nfull, lambda j, c: step(j, c, False), tuple(_flash_init(tq) for _ in range(8)))
    carry = lax.fori_loop(nfull, nall, lambda j, c: step(j, c, True), carry)
    chunks = _diff_finish(carry, par_ref[...], tq)
    for g in range(2):
        o_ref[:, g * LANES:(g + 1) * LANES] = chunks[g].astype(BF16)


def _diff_prompt(dq, dk, dv, par, tq=128, tk=256):
    b, t, _ = dq.shape
    return pl.pallas_call(
        functools.partial(_diff_kernel, tq=tq, tk=tk),
        grid=(b, t // tq),
        in_specs=[
            pl.BlockSpec((None, tq, 256), lambda i, q: (i, q, 0)),
            pl.BlockSpec((None, t, LANES), lambda i, q: (i, 0, 0)),
            pl.BlockSpec((None, t, LANES), lambda i, q: (i, 0, 0)),
            pl.BlockSpec((8, LANES), lambda i, q: (0, 0)),
        ],
        out_specs=pl.BlockSpec((None, tq, 256), lambda i, q: (i, q, 0)),
        out_shape=jax.ShapeDtypeStruct((b, t, 256), BF16),
        compiler_params=_params(("parallel", "arbitrary")), name="diff_prompt",
    )(dq, dk, dv, par)


def _nsa_queries(q_ref, rows):
    lane = _lane_iota((rows, LANES))
    qch = [q_ref[:, c * LANES:(c + 1) * LANES].astype(F32) for c in range(3)]
    out = []
    for g in range(2):
        glanes = (lane >= g * HEAD_DIM) & (lane < (g + 1) * HEAD_DIM)
        parts = []
        for i in range(3):
            h = 3 * g + i
            src = qch[h // 2]
            if (h % 2) != g:
                src = pltpu.roll(src, HEAD_DIM, 1)
            parts.append(jnp.where(glanes, src, 0.0).astype(BF16))
        out.append(jnp.concatenate(parts, axis=0))
    return out


def _tile3(x):
    return jnp.concatenate([x, x, x], axis=0)


def _nsa_combine(o_ref, gate, branches, rows):
    lane = _lane_iota((rows, LANES))
    heads = [None] * NSA_HEADS
    for g in range(2):
        oc, osel, ow = branches[g]
        for i in range(3):
            h = 3 * g + i
            sl = slice(i * rows, (i + 1) * rows)
            o = (gate[:, 6 + h:7 + h] * oc[sl] + gate[:, 12 + h:13 + h] * osel[sl]
                 + gate[:, 18 + h:19 + h] * ow[sl])
            if (h % 2) != g:
                o = pltpu.roll(o, HEAD_DIM, 1)
            heads[h] = o
    for c in range(3):
        o_ref[:, c * LANES:(c + 1) * LANES] = jnp.where(lane < HEAD_DIM, heads[2 * c], heads[2 * c + 1]).astype(BF16)


def _block_expand(off, tk):
    n = lax.broadcasted_iota(I32, (LANES, tk), 0)
    k = off + lax.broadcasted_iota(I32, (LANES, tk), 1)
    return jnp.where(n == (k >> 6), 1.0, 0.0).astype(BF16)


def _nsa_kernel(q_ref, cmpk_ref, cmpv_ref, sk_ref, sv_ref, wk_ref, wv_ref, gate_ref, o_ref, *, tq, tk, nb):
    q0 = pl.program_id(1) * tq
    qgs = _nsa_queries(q_ref, tq)
    cmpk = cmpk_ref[...].astype(BF16)
    cmpv = cmpv_ref[...].astype(BF16)
    rows = q0 + lax.broadcasted_iota(I32, (tq, tk), 0)
    cols = lax.broadcasted_iota(I32, (tq, tk), 1)
    t_qn = _tile3(q0 + lax.broadcasted_iota(I32, (tq, nb), 0))
    n_qn = _lane_iota((3 * tq, nb))
    vis = (n_qn + 1) * NSA_BLOCK - 1 <= t_qn
    n_t = lax.broadcasted_iota(I32, (nb, tq), 0)
    t_t = q0 + lax.broadcasted_iota(I32, (nb, tq), 1)
    vis_t = (n_t + 1) * NSA_BLOCK - 1 <= t_t
    cur_t = t_t >> 6
    forced_t = (n_t == 0) | (n_t == cur_t) | (n_t == cur_t - 1)
    n_tf = n_t.astype(F32)

    branches = []
    for g in range(2):
        qg = qgs[g]
        pc = _masked_softmax(_nt_dot(qg, cmpk), vis, 1)
        o_cmp = _dot(pc.astype(BF16), cmpv)
        sc_t = _nt_dot(cmpk, qg)
        imp = None
        for i in range(3):
            pt = _masked_softmax(sc_t[:, i * tq:(i + 1) * tq], vis_t, 0)
            imp = pt if imp is None else imp + pt
        imp = jnp.where(forced_t, FORCED_SCORE, imp)
        imp = jnp.where(n_t <= cur_t, imp, -1.0)
        sel_t = _topk_mask(imp, n_tf, NSA_TOPK, 0)
        if nb < LANES:
            sel_t = jnp.concatenate([sel_t, jnp.zeros((LANES - nb, tq), F32)], axis=0)
        sel = sel_t.T.astype(BF16)

        def sel_step(j, carry, diag):
            off = pl.multiple_of(j * tk, tk)
            hit = _dot(sel, _block_expand(off, tk)) > 0.5
            if diag:
                hit = hit & (rows >= off + cols)
            mask = _tile3(jnp.where(hit, 1.0, 0.0)) > 0.5
            m, l, acc = carry
            s = _nt_dot(qg, sk_ref[pl.ds(off, tk), :])
            p, alpha, m, l = _softmax_step(s, mask, m, l)
            return m, l, alpha * acc + _dot(p.astype(BF16), sv_ref[pl.ds(off, tk), :])

        nfull, nall = _causal_bounds(q0, tq, tk)
        carry = lax.fori_loop(0, nfull, lambda j, c: sel_step(j, c, False), _flash_init(3 * tq))
        _, l_s, a_s = lax.fori_loop(nfull, nall, lambda j, c: sel_step(j, c, True), carry)

        def win_step(j, carry):
            off = pl.multiple_of(j * tk, tk)
            dist = rows - (off + cols)
            mask = _tile3(jnp.where((dist >= 0) & (dist <= WINDOW), 1.0, 0.0)) > 0.5
            m, l, acc = carry
            s = _nt_dot(qg, wk_ref[pl.ds(off, tk), :])
            p, alpha, m, l = _softmax_step(s, mask, m, l)
            return m, l, alpha * acc + _dot(p.astype(BF16), wv_ref[pl.ds(off, tk), :])

        jlo = jnp.maximum(q0 - WINDOW, 0) // tk
        _, l_w, a_w = lax.fori_loop(jlo, nall, win_step, _flash_init(3 * tq))
        branches.append((o_cmp, _normalise(a_s, l_s), _normalise(a_w, l_w)))
    _nsa_combine(o_ref, gate_ref[...], branches, tq)


def _nsa_prompt(nq, cmpk, cmpv, sk, sv, wk, wv, gate, tq=128, tk=256):
    b, t, _ = nq.shape
    nb = t // NSA_BLOCK
    full = pl.BlockSpec((None, t, LANES), lambda i, q: (i, 0, 0))
    blk = pl.BlockSpec((None, nb, LANES), lambda i, q: (i, 0, 0))
    return pl.pallas_call(
        functools.partial(_nsa_kernel, tq=tq, tk=tk, nb=nb),
        grid=(b, t // tq),
        in_specs=[pl.BlockSpec((None, tq, 384), lambda i, q: (i, q, 0)), blk, blk, full, full, full, full,
                  pl.BlockSpec((None, tq, LANES), lambda i, q: (i, q, 0))],
        out_specs=pl.BlockSpec((None, tq, 384), lambda i, q: (i, q, 0)),
        out_shape=jax.ShapeDtypeStruct((b, t, 384), BF16),
        compiler_params=_params(("parallel", "arbitrary")), name="nsa_prompt",
    )(nq, cmpk, cmpv, sk, sv, wk, wv, gate)


ROWS = 8


def _paged_specs(block, p_per_step):
    def make(i):
        return pl.BlockSpec((None, None) + block,
                            lambda b, j, pt, lr: (lr[0], pt[b, j * p_per_step + i]) + (0,) * len(block))
    return [make(i) for i in range(p_per_step)]


def _seq_spec(block):
    return pl.BlockSpec((None,) + block, lambda b, j, pt, lr: (b,) + (0,) * len(block))


def _sample_call(kern, name, page_table, lidx, in_specs, out_specs, out_shape, scratch, args, p_per_step):
    bs, n_pages = page_table.shape
    return pl.pallas_call(
        kern,
        grid_spec=pltpu.PrefetchScalarGridSpec(
            num_scalar_prefetch=2, grid=(bs, n_pages // p_per_step),
            in_specs=in_specs, out_specs=out_specs, scratch_shapes=scratch),
        out_shape=out_shape, compiler_params=_params(("parallel", "arbitrary")), name=name,
    )(page_table, lidx, *args)


def _row_tile(n):
    t = lax.broadcasted_iota(I32, (n * ROWS, PAGE_SIZE), 0) & (ROWS - 1)
    c = lax.broadcasted_iota(I32, (n * ROWS, PAGE_SIZE), 1)
    return t, c


def _gather_logf_kernel(pt_ref, l_ref, *refs, P):
    o_ref = refs[P]
    for i in range(P):
        o_ref[i * PAGE_SIZE:(i + 1) * PAGE_SIZE, :] = refs[i][...]


def _gather_logf(page_table, lidx, cache_logf, P):
    bs, n_pages = page_table.shape
    past = n_pages * PAGE_SIZE
    return _sample_call(
        functools.partial(_gather_logf_kernel, P=P), "gather_logf", page_table, lidx,
        _paged_specs((PAGE_SIZE, FOX_HEADS), P),
        pl.BlockSpec((None, P * PAGE_SIZE, FOX_HEADS), lambda b, j, pt, lr: (b, j, 0)),
        jax.ShapeDtypeStruct((bs, past, FOX_HEADS), F32), [], (cache_logf,) * P, P)


class _Flash:
    def __init__(self, m_s, l_s, a_s, rows):
        self.m_s, self.l_s, self.a_s, self.rows = m_s, l_s, a_s, rows

    def init(self):
        self.m_s[...] = jnp.full(self.m_s.shape, NEG, F32)
        self.l_s[...] = jnp.zeros(self.l_s.shape, F32)
        self.a_s[...] = jnp.zeros(self.a_s.shape, F32)

    def update(self, idx, s, mask, pv):
        rs = slice(idx * self.rows, (idx + 1) * self.rows)
        p, alpha, m, l = _softmax_step(s, mask, self.m_s[rs, 0:1], self.l_s[rs, 0:1])
        self.a_s[rs, :] = alpha * self.a_s[rs, :] + pv(p.astype(BF16))
        self.m_s[rs, :] = jnp.broadcast_to(m, (self.rows, LANES))
        self.l_s[rs, :] = jnp.broadcast_to(l, (self.rows, LANES))

    def result(self, idx, sub=None):
        rs = slice(idx * self.rows, (idx + 1) * self.rows)
        out = _normalise(self.a_s[rs, :], self.l_s[rs, 0:1])
        return out if sub is None else out[sub]


def _flash_scratch(groups, rows):
    return [pltpu.VMEM((groups * rows, LANES), F32), pltpu.VMEM((groups * rows, LANES), F32),
            pltpu.VMEM((groups * rows, HEAD_DIM), F32)]


def _page_pv(p, v_pages):
    out = None
    for i, v in enumerate(v_pages):
        term = _dot(p[:, i * PAGE_SIZE:(i + 1) * PAGE_SIZE], v)
        out = term if out is None else out + term
    return out


def _fox_sample_kernel(pt_ref, l_ref, q_ref, *refs, P, ts):
    k_refs, v_refs = refs[:P], refs[P:2 * P]
    kc_ref, kcn_ref, qc_ref, kn_ref, vn_ref, o_ref, m_s, l_s, a_s = refs[2 * P:]
    j = pl.program_id(1)
    fl = _Flash(m_s, l_s, a_s, ROWS)
    pl.when(j == 0)(fl.init)
    q = q_ref[...].astype(F32)
    qc = qc_ref[...]
    qh = [q[:, h * HEAD_DIM:(h + 1) * HEAD_DIM].astype(BF16) for h in range(FOX_HEADS)]

    for h in range(FOX_HEADS):
        s = jnp.concatenate([_nt_dot(qh[h], k_refs[i][:, h, :].astype(BF16)) for i in range(P)], axis=1)
        s = s + (qc[:, h:h + 1] - kc_ref[h:h + 1, :])
        fl.update(h, s, None, lambda p, h=h: _page_pv(p, [v_refs[i][:, h, :].astype(BF16) for i in range(P)]))

    @pl.when(j == pl.num_programs(1) - 1)
    def _():
        t, c = _row_tile(1)
        mask = (c <= t) & (c < ts)
        for h in range(FOX_HEADS):
            s = _nt_dot(qh[h], kn_ref[:, h, :].astype(BF16)) + (qc[:, h:h + 1] - kcn_ref[h:h + 1, :])
            fl.update(h, s, mask, lambda p, h=h: _dot(p, vn_ref[:, h, :].astype(BF16)))
            o_ref[:, h * HEAD_DIM:(h + 1) * HEAD_DIM] = fl.result(h)


def _fox_sample(page_table, lidx, q, cache_k, cache_v, c_rows, qc, k_new, v_new, P, ts):
    bs, n_pages = page_table.shape
    page = (PAGE_SIZE, FOX_HEADS, HEAD_DIM)
    in_specs = ([_seq_spec((ROWS, 384))] + _paged_specs(page, P) + _paged_specs(page, P) + [
        pl.BlockSpec((None, 8, P * PAGE_SIZE), lambda b, j, pt, lr: (b, 0, j)),
        pl.BlockSpec((None, 8, PAGE_SIZE), lambda b, j, pt, lr: (b, 0, n_pages)),
        _seq_spec((ROWS, 8)), _seq_spec(page), _seq_spec(page)])
    return _sample_call(
        functools.partial(_fox_sample_kernel, P=P, ts=ts), "fox_sample", page_table, lidx, in_specs,
        _seq_spec((ROWS, 384)), jax.ShapeDtypeStruct((bs, ROWS, 384), F32), _flash_scratch(FOX_HEADS, ROWS),
        (q,) + (cache_k,) * P + (cache_v,) * P + (c_rows, c_rows, qc, k_new, v_new), P)


def _diff_sample_kernel(pt_ref, l_ref, q_ref, *refs, P, ts):
    k_refs, v_refs = refs[:P], refs[P:2 * P]
    par_ref, kn_ref, vn_ref, o_ref, m_s, l_s, a_s = refs[2 * P:]
    j = pl.program_id(1)
    fl = _Flash(m_s, l_s, a_s, 2 * ROWS)
    pl.when(j == 0)(fl.init)
    q = q_ref[...].astype(F32)
    scale = DIFF_SUB ** -0.5

    def qvar(g, c):
        cols = [(2 * g + r) * HEAD_DIM + c * DIFF_SUB for r in range(2)]
        return jnp.concatenate([q[:, lo:lo + DIFF_SUB] for lo in cols], axis=0).astype(BF16)

    qv = [[qvar(g, c) for c in range(2)] for g in range(2)]
    for g in range(2):
        for c in range(2):
            s = jnp.concatenate([_nt_dot(qv[g][c], k_refs[i][:, g, c, :].astype(BF16)) for i in range(P)],
                                axis=1) * scale
            fl.update(2 * g + c, s, None,
                      lambda p, g=g: _page_pv(p, [v_refs[i][:, g, :].astype(BF16) for i in range(P)]))

    @pl.when(j == pl.num_programs(1) - 1)
    def _():
        t, c_idx = _row_tile(2)
        mask = (c_idx <= t) & (c_idx < ts)
        par = par_ref[...]
        lam, lam_init = _diff_lambda(par)
        for g in range(2):
            for c in range(2):
                s = _nt_dot(qv[g][c], kn_ref[:, g, c, :].astype(BF16)) * scale
                fl.update(2 * g + c, s, mask, lambda p, g=g: _dot(p, vn_ref[:, g, :].astype(BF16)))
            for r in range(2):
                sub = slice(r * ROWS, (r + 1) * ROWS)
                o = fl.result(2 * g, sub) - lam * fl.result(2 * g + 1, sub)
                y = o * lax.rsqrt(jnp.mean(o * o, axis=-1, keepdims=True) + EPS) * par[5:6, 0:HEAD_DIM]
                h = 2 * g + r
                o_ref[:, h * HEAD_DIM:(h + 1) * HEAD_DIM] = y * (1.0 - lam_init)


def _diff_sample(page_table, lidx, q, cache_k, cache_v, par, k_new, v_new, P, ts):
    bs, _ = page_table.shape
    kpage, vpage = (PAGE_SIZE, 2, 2, DIFF_SUB), (PAGE_SIZE, 2, HEAD_DIM)
    in_specs = ([_seq_spec((ROWS, 256))] + _paged_specs(kpage, P) + _paged_specs(vpage, P) + [
        pl.BlockSpec((8, LANES), lambda b, j, pt, lr: (0, 0)), _seq_spec(kpage), _seq_spec(vpage)])
    return _sample_call(
        functools.partial(_diff_sample_kernel, P=P, ts=ts), "diff_sample", page_table, lidx, in_specs,
        _seq_spec((ROWS, 256)), jax.ShapeDtypeStruct((bs, ROWS, 256), F32), _flash_scratch(4, 2 * ROWS),
        (q,) + (cache_k,) * P + (cache_v,) * P + (par, k_new, v_new), P)


def _nsa_group_q(q, g):
    return jnp.concatenate([q[:, (3 * g + i) * HEAD_DIM:(3 * g + i + 1) * HEAD_DIM] for i in range(3)],
                           axis=0).astype(BF16)


def _nsa_s1_kernel(pt_ref, l_ref, q_ref, *refs, P, ts, past, nbp):
    ck_refs, cv_refs = refs[:P], refs[P:2 * P]
    (w_ref, ckn_ref, cvn_ref, wkb_ref, wvb_ref, wkn_ref, wvn_ref,
     ocmp_ref, owin_ref, sel_ref, cmpk_s, cmpv_s) = refs[2 * P:]
    j = pl.program_id(1)

    @pl.when(j == 0)
    def _():
        cmpk_s[...] = jnp.zeros(cmpk_s.shape, F32)
        cmpv_s[...] = jnp.zeros(cmpv_s.shape, F32)

    wk, wv = w_ref[0], w_ref[1]
    nstep = 2 * P
    inv = 1.0 / NSA_BLOCK

    def summarise(pages, w):
        x = jnp.concatenate([pg * w for pg in pages], axis=0)
        return jnp.sum(x.reshape(x.shape[0] // NSA_BLOCK, NSA_BLOCK, HEAD_DIM), axis=1) * inv

    for g in range(2):
        rows = pl.ds(pl.multiple_of(j * nstep, nstep), nstep)
        cmpk_s[g, rows, :] = summarise([ck_refs[i][:, g, :] for i in range(P)], wk)
        cmpv_s[g, rows, :] = summarise([cv_refs[i][:, g, :] for i in range(P)], wv)

    @pl.when(j == pl.num_programs(1) - 1)
    def _():
        q = q_ref[...].astype(F32)
        nb_past = past // NSA_BLOCK
        nb = (past + ts + NSA_BLOCK - 1) // NSA_BLOCK
        win_buf = wkb_ref.shape[0]
        for g in range(2):
            cmpk_s[g, nb_past:nb_past + 2, :] = summarise([ckn_ref[:, g, :]], wk)
            cmpv_s[g, nb_past:nb_past + 2, :] = summarise([cvn_ref[:, g, :]], wv)
            qg = _nsa_group_q(q, g)
            sc = _nt_dot(qg, cmpk_s[g].astype(BF16))
            n = _lane_iota((3 * ROWS, nbp))
            pos = past + (lax.broadcasted_iota(I32, (3 * ROWS, nbp), 0) & (ROWS - 1))
            pc = _masked_softmax(sc, ((n + 1) * NSA_BLOCK - 1 <= pos) & (n < nb), 1)
            ocmp_ref[g] = _dot(pc.astype(BF16), cmpv_s[g].astype(BF16))
            imp = pc[0:ROWS] + pc[ROWS:2 * ROWS] + pc[2 * ROWS:3 * ROWS]
            n8 = _lane_iota((ROWS, nbp))
            cur = (past + lax.broadcasted_iota(I32, (ROWS, nbp), 0)) >> 6
            imp = jnp.where((n8 == 0) | (n8 == cur) | (n8 == cur - 1), FORCED_SCORE, imp)
            imp = jnp.where(n8 <= cur, imp, -1.0)
            sel_ref[g] = _topk_mask(imp, n8.astype(F32), NSA_TOPK, 1)
            s = jnp.concatenate([_nt_dot(qg, wkb_ref[:, g, :].astype(BF16)),
                                 _nt_dot(qg, wkn_ref[:, g, :].astype(BF16))], axis=1)
            width = win_buf + PAGE_SIZE
            i_idx = _lane_iota((3 * ROWS, width))
            t = lax.broadcasted_iota(I32, (3 * ROWS, width), 0) & (ROWS - 1)
            in_buf = (i_idx < win_buf) & (i_idx >= win_buf + t - WINDOW) & (i_idx >= win_buf - past)
            c_new = i_idx - win_buf
            in_new = (c_new >= 0) & (c_new <= t) & (c_new < ts)
            p = _masked_softmax(s, in_buf | in_new, 1).astype(BF16)
            owin_ref[g] = (_dot(p[:, 0:win_buf], wvb_ref[:, g, :].astype(BF16))
                           + _dot(p[:, win_buf:width], wvn_ref[:, g, :].astype(BF16)))


def _nsa_s1(page_table, lidx, q, cache_ck, cache_cv, cmp_w, ck_new, cv_new, win_k, win_v, wk_new, wv_new,
            P, ts, nbp):
    bs, n_pages = page_table.shape
    page = (PAGE_SIZE, 2, HEAD_DIM)
    win_buf = win_k.shape[2]
    win_spec = pl.BlockSpec((None, None, win_buf, 2, HEAD_DIM), lambda b, j, pt, lr: (lr[0], b, 0, 0, 0))
    in_specs = ([_seq_spec((ROWS, 384))] + _paged_specs(page, P) + _paged_specs(page, P) + [
        pl.BlockSpec((2, PAGE_SIZE, HEAD_DIM), lambda b, j, pt, lr: (0, 0, 0)),
        _seq_spec(page), _seq_spec(page), win_spec, win_spec, _seq_spec(page), _seq_spec(page)])
    o_spec = _seq_spec((2, 3 * ROWS, HEAD_DIM))
    o_shape = jax.ShapeDtypeStruct((bs, 2, 3 * ROWS, HEAD_DIM), F32)
    return _sample_call(
        functools.partial(_nsa_s1_kernel, P=P, ts=ts, past=n_pages * PAGE_SIZE, nbp=nbp), "nsa_sample_cmp",
        page_table, lidx, in_specs, [o_spec, o_spec, _seq_spec((2, ROWS, nbp))],
        [o_shape, o_shape, jax.ShapeDtypeStruct((bs, 2, ROWS, nbp), F32)],
        [pltpu.VMEM((2, nbp, HEAD_DIM), F32), pltpu.VMEM((2, nbp, HEAD_DIM), F32)],
        (q,) + (cache_ck,) * P + (cache_cv,) * P + (cmp_w, ck_new, cv_new, win_k, win_v, wk_new, wv_new), P)


def _nsa_s2_kernel(pt_ref, l_ref, q_ref, *refs, P, ts, past, nbp):
    sk_refs, sv_refs = refs[:P], refs[P:2 * P]
    sel_ref, skn_ref, svn_ref, ocmp_ref, owin_ref, gate_ref, o_ref, m_s, l_s, a_s = refs[2 * P:]
    j = pl.program_id(1)
    fl = _Flash(m_s, l_s, a_s, 3 * ROWS)
    pl.when(j == 0)(fl.init)
    q = q_ref[...].astype(F32)
    tk = P * PAGE_SIZE
    n = lax.broadcasted_iota(I32, (nbp, tk), 0)
    k = j * tk + lax.broadcasted_iota(I32, (nbp, tk), 1)
    expand = jnp.where(n == (k >> 6), 1.0, 0.0).astype(BF16)
    qgs = [_nsa_group_q(q, g) for g in range(2)]
    for g in range(2):
        s = jnp.concatenate([_nt_dot(qgs[g], sk_refs[i][:, g, :].astype(BF16)) for i in range(P)], axis=1)
        hit = _dot(sel_ref[g].astype(BF16), expand)
        fl.update(g, s, _tile3(hit) > 0.5,
                  lambda p, g=g: _page_pv(p, [sv_refs[i][:, g, :].astype(BF16) for i in range(P)]))

    @pl.when(j == pl.num_programs(1) - 1)
    def _():
        t, c = _row_tile(1)
        nb_past = past // NSA_BLOCK
        gate = gate_ref[...]
        for g in range(2):
            chosen = sel_ref[g][:, nb_past:nb_past + 1] > 0.5
            hit = jnp.where(chosen & (c <= t) & (c < ts), 1.0, 0.0)
            s = _nt_dot(qgs[g], skn_ref[:, g, :].astype(BF16))
            fl.update(g, s, _tile3(hit) > 0.5, lambda p, g=g: _dot(p, svn_ref[:, g, :].astype(BF16)))
            for i in range(3):
                h = 3 * g + i
                sub = slice(i * ROWS, (i + 1) * ROWS)
                o = (gate[:, 6 + h:7 + h] * ocmp_ref[g, sub, :] + gate[:, 12 + h:13 + h] * fl.result(g, sub)
                     + gate[:, 18 + h:19 + h] * owin_ref[g, sub, :])
                o_ref[:, h * HEAD_DIM:(h + 1) * HEAD_DIM] = o


def _nsa_s2(page_table, lidx, q, cache_sk, cache_sv, sel, sk_new, sv_new, ocmp, owin, gate, P, ts, nbp):
    bs, n_pages = page_table.shape
    page = (PAGE_SIZE, 2, HEAD_DIM)
    o_spec = _seq_spec((2, 3 * ROWS, HEAD_DIM))
    in_specs = ([_seq_spec((ROWS, 384))] + _paged_specs(page, P) + _paged_specs(page, P) + [
        _seq_spec((2, ROWS, nbp)), _seq_spec(page), _seq_spec(page), o_spec, o_spec, _seq_spec((ROWS, LANES))])
    return _sample_call(
        functools.partial(_nsa_s2_kernel, P=P, ts=ts, past=n_pages * PAGE_SIZE, nbp=nbp), "nsa_sample_sel",
        page_table, lidx, in_specs, _seq_spec((ROWS, 384)), jax.ShapeDtypeStruct((bs, ROWS, 384), F32),
        _flash_scratch(2, 3 * ROWS),
        (q,) + (cache_sk,) * P + (cache_sv,) * P + (sel, sk_new, sv_new, ocmp, owin, gate), P)


def _rope_tables(pos, dim):
    inv = jnp.exp(-math.log(ROPE_THETA) * jnp.arange(0, dim, 2, dtype=F32) / dim)
    ang = pos.astype(F32)[:, None] * inv[None, :]
    cos, sin = jnp.cos(ang), jnp.sin(ang)
    reps = LANES // dim
    return (jnp.tile(jnp.concatenate([cos, cos], axis=1), (1, reps)),
            jnp.tile(jnp.concatenate([-sin, sin], axis=1), (1, reps)))


def _layer_weights(l, w_in, b_fox_f, fox_qn, fox_kn, diff_qn, diff_kn, diff_lambda, diff_subln, nsa_qn,
                   nsa_kn, nsa_cmp_w, nsa_gate_b, w_o, norm1, norm2, w_ff1, w_ff2, tm_max):
    fq, fk, fv, ff, dq, dk, dv, nq, nkv, ng = jnp.split(w_in[l], SPLIT_POINTS, axis=1)
    small = jnp.concatenate([ff, ng, jnp.zeros((D_MODEL, LANES - 24), F32)], axis=1)
    w_packed = jnp.concatenate([fq, fk, fv, dq, dk, dv, nq, nkv, small], axis=1).astype(BF16)
    t2 = lambda v: jnp.tile(v, LANES // v.shape[0])
    gains = jnp.stack([t2(fox_qn[l]), t2(fox_kn[l]), t2(diff_qn[l]), t2(diff_kn[l]), t2(nsa_qn[l]),
                       t2(nsa_kn[l, 0]), t2(nsa_kn[l, 1]), t2(nsa_kn[l, 2])], axis=0)
    bias_small = jnp.concatenate([b_fox_f[l], nsa_gate_b[l], jnp.zeros((LANES - 24,), F32)])[None, :]
    lam_init = 0.8 - 0.6 * math.exp(-0.3 * l)
    dpar = jnp.zeros((8, LANES), F32)
    dpar = dpar.at[0:4, 0:DIFF_SUB].set(diff_lambda[l].astype(F32))
    dpar = dpar.at[4, :].set(lam_init)
    dpar = dpar.at[5, :].set(t2(diff_subln[l]))
    reps = tm_max // NSA_BLOCK
    return dict(
        w_in=w_packed, gains=gains, bias_small=bias_small, dpar=dpar,
        g1=norm1[l][None, :], g2=norm2[l][None, :],
        wck=jnp.tile(jnp.tile(nsa_cmp_w[l, 0], (1, 2)), (reps, 1)),
        wcv=jnp.tile(jnp.tile(nsa_cmp_w[l, 1], (1, 2)), (reps, 1)),
        cmp_w=jnp.tile(nsa_cmp_w[l], (1, 2, 1)),
        w_o=w_o[l].astype(BF16), w_ff1=w_ff1[l].astype(BF16), w_ff2=w_ff2[l].astype(BF16),
    )


def _prompt_layer(x, lw, tabs, b, t, tm):
    pr = _project(x, lw, tabs, tm)
    r3 = lambda a: a.reshape(b, t, a.shape[-1])
    logf_rows = jnp.swapaxes(r3(pr["logf"])[:, :, 0:8], 1, 2)
    c_rows = _cumsum_rows(logf_rows.reshape(b * 8, t), b * 8).reshape(b, 8, t)[:, 0:FOX_HEADS]
    kc = jnp.pad(c_rows.reshape(b, 3, 2, t), ((0, 0), (0, 0), (0, 6), (0, 0)))
    qc = jnp.swapaxes(kc, 2, 3)
    o_a = _fox_prompt(r3(pr["fq_b"]), r3(pr["fk_b"]), r3(pr["fv_b"]), qc, kc)
    o_b = _diff_prompt(r3(pr["dq_b"]), r3(pr["dk_b"]), r3(pr["dv_b"]), lw["dpar"])
    nb = t // NSA_BLOCK
    o_c = _nsa_prompt(r3(pr["nq_b"]), pr["cmpk"].reshape(b, nb, LANES), pr["cmpv"].reshape(b, nb, LANES),
                      r3(pr["sk_b"]), r3(pr["sv_b"]), r3(pr["wk_b"]), r3(pr["wv_b"]), r3(pr["gate"]))
    x_out = _post(x, o_a.reshape(b * t, 384), o_b.reshape(b * t, 256), o_c.reshape(b * t, 384), lw, tm)
    return x_out, pr


PAGES_PER_STEP = 8
CUMSUM_TILE = 512


def _sample_layer(x, lw, tabs, caches, page_table, l, bs, ts, p_step=PAGES_PER_STEP):
    (c_fk, c_fv, c_fl, c_dk, c_dv, c_ck, c_cv, c_sk, c_sv, s_wk, s_wv) = caches
    n_pages = page_table.shape[1]
    past = n_pages * PAGE_SIZE
    lidx = jnp.full((1,), l, I32)
    pr = _project(x, lw, tabs, bs * ts)
    seq = lambda a, tail: a.reshape((bs, ts) + tail)
    pad_q = lambda a: jnp.pad(seq(a, a.shape[1:]), ((0, 0), (0, ROWS - ts), (0, 0)))
    def new_page(a, tail):
        a = seq(a, tail)
        return jnp.pad(a, ((0, 0), (0, PAGE_SIZE - ts)) + ((0, 0),) * len(tail))

    lf_past = jnp.swapaxes(_gather_logf(page_table, lidx, c_fl, p_step), 1, 2)
    lf_new = jnp.swapaxes(seq(pr["logf"], (LANES,))[:, :, 0:8], 1, 2)
    rows_all = jnp.concatenate([jnp.pad(lf_past, ((0, 0), (0, 2), (0, 0))),
                                jnp.pad(lf_new, ((0, 0), (0, 0), (0, CUMSUM_TILE - ts)))], axis=2)
    width = past + CUMSUM_TILE
    c_rows = _cumsum_rows(rows_all.reshape(bs * 8, width), min(bs * 8, 128), CUMSUM_TILE).reshape(bs, 8, width)
    qc = jnp.swapaxes(c_rows[:, :, past:past + ROWS], 1, 2)

    o_a = _fox_sample(page_table, lidx, pad_q(pr["fq_b"]), c_fk, c_fv, c_rows, qc,
                      new_page(pr["fk_f"], (FOX_HEADS, HEAD_DIM)), new_page(pr["fv_f"], (FOX_HEADS, HEAD_DIM)),
                      p_step, ts)
    o_b = _diff_sample(page_table, lidx, pad_q(pr["dq_b"]), c_dk, c_dv, lw["dpar"],
                       new_page(pr["dk_f"], (2, 2, DIFF_SUB)), new_page(pr["dv_f"], (2, HEAD_DIM)), p_step, ts)
    nq = pad_q(pr["nq_b"])
    kv = lambda name: new_page(pr[name], (2, HEAD_DIM))
    nbp = -(-(past // NSA_BLOCK + 2) // LANES) * LANES
    ocmp, owin, sel = _nsa_s1(page_table, lidx, nq, c_ck, c_cv, lw["cmp_w"], kv("ck_f"), kv("cv_f"),
                              s_wk, s_wv, kv("wk_f"), kv("wv_f"), p_step, ts, nbp)
    o_c = _nsa_s2(page_table, lidx, nq, c_sk, c_sv, sel, kv("sk_f"), kv("sv_f"), ocmp, owin,
                  pad_q(pr["gate"]), p_step, ts, nbp)
    unpad = lambda o: o[:, 0:ts].reshape(bs * ts, o.shape[-1])
    x_out = _post(x, unpad(o_a), unpad(o_b), unpad(o_c), lw, bs * ts)
    return x_out, pr


def kernel(x_prompt, x_sample, cache_fox_k, cache_fox_v, cache_fox_logf, cache_diff_k, cache_diff_v,
           cache_nsa_cmp_k, cache_nsa_cmp_v, cache_nsa_sel_k, cache_nsa_sel_v, state_nsa_win_k,
           state_nsa_win_v, page_table, w_in, b_fox_f, fox_qn, fox_kn, diff_qn, diff_kn, diff_lambda,
           diff_subln, nsa_qn, nsa_kn, nsa_cmp_w, nsa_gate_b, w_o, norm1, norm2, w_ff1, w_ff2):
    b, t, _ = x_prompt.shape
    bs, ts, _ = x_sample.shape
    depth = w_in.shape[0]
    past = page_table.shape[1] * PAGE_SIZE
    win_buf = state_nsa_win_k.shape[2]
    tm = 256
    pos = jnp.arange(t, dtype=I32)
    tabs_p = _rope_tables(pos, HEAD_DIM) + _rope_tables(pos, DIFF_SUB)
    pos_s = jnp.tile(past + jnp.arange(ts, dtype=I32), bs)
    tabs_s = _rope_tables(pos_s, HEAD_DIM) + _rope_tables(pos_s, DIFF_SUB)
    caches = (cache_fox_k, cache_fox_v, cache_fox_logf, cache_diff_k, cache_diff_v, cache_nsa_cmp_k,
              cache_nsa_cmp_v, cache_nsa_sel_k, cache_nsa_sel_v, state_nsa_win_k, state_nsa_win_v)
    xp = x_prompt.reshape(b * t, D_MODEL)
    xs = x_sample.reshape(bs * ts, D_MODEL)
    new_p = [[] for _ in range(11)]
    new_s = [[] for _ in range(11)]
    for l in range(depth):
        lw = _layer_weights(l, w_in, b_fox_f, fox_qn, fox_kn, diff_qn, diff_kn, diff_lambda, diff_subln,
                            nsa_qn, nsa_kn, nsa_cmp_w, nsa_gate_b, w_o, norm1, norm2, w_ff1, w_ff2, tm)
        xp, pr = _prompt_layer(xp, lw, tabs_p, b, t, tm)
        xs, ps = _sample_layer(xs, lw, tabs_s, caches, page_table, l, bs, ts)
        for grp, rows, lead in ((new_p, pr, (b, t)), (new_s, ps, (bs, ts))):
            shp = lambda a, tail: a.reshape(lead + tail)
            grp[0].append(shp(rows["fk_f"], (FOX_HEADS, HEAD_DIM)))
            grp[1].append(shp(rows["fv_f"], (FOX_HEADS, HEAD_DIM)))
            grp[2].append(shp(rows["logf"], (LANES,))[..., 0:FOX_HEADS])
            grp[3].append(shp(rows["dk_f"], (2, 2, DIFF_SUB)))
            grp[4].append(shp(rows["dv_f"], (2, HEAD_DIM)))
            for n, name in enumerate(("ck_f", "cv_f", "sk_f", "sv_f", "wk_f", "wv_f")):
                grp[5 + n].append(shp(rows[name], (2, HEAD_DIM)))
        for n, state in ((9, state_nsa_win_k), (10, state_nsa_win_v)):
            new_p[n][-1] = jnp.pad(new_p[n][-1], ((0, 0), (win_buf, 0), (0, 0), (0, 0)))[:, -win_buf:]
            new_s[n][-1] = jnp.concatenate([state[l], new_s[n][-1]], axis=1)[:, -win_buf:]
    outs = [xp.reshape(b, t, D_MODEL), xs.reshape(bs, ts, D_MODEL)]
    for n in range(11):
        outs.append(jnp.stack(new_p[n], axis=0))
        outs.append(jnp.stack(new_s[n], axis=0))
    return tuple(outs)
```

```python
import functools
import math

import jax
import jax.numpy as jnp
from jax import lax
from jax.experimental import pallas as pl
from jax.experimental.pallas import tpu as pltpu

F32 = jnp.float32
BF16 = jnp.bfloat16
I32 = jnp.int32

D_MODEL = 1024
HEAD_DIM = 64
FOX_HEADS = 6
DIFF_HEADS = 4
NSA_HEADS = 6
DIFF_SUB = 32
D_FF = 4 * D_MODEL
ROPE_THETA = 10000.0
NSA_BLOCK = 64
NSA_TOPK = 16
WINDOW = 512
PAGE_SIZE = 128
FORCED_SCORE = 1.0e4
EPS = 1e-6
NEG = -1.0e30
LANES = 128

C_FQ, C_FK, C_FV = 0, 384, 768
C_DQ, C_DK, C_DV = 1152, 1408, 1536
C_NQ, C_NKV, C_SMALL = 1664, 2048, 2816
W_PACKED = 2944
SPLITS = (384, 384, 384, 6, 256, 128, 128, 384, 768, 18)
SPLIT_POINTS = tuple(int(sum(SPLITS[:i + 1])) for i in range(len(SPLITS) - 1))

VMEM_LIMIT = 56 * 1024 * 1024


def _params(sem):
    return pltpu.CompilerParams(dimension_semantics=sem, vmem_limit_bytes=VMEM_LIMIT)


def _lane_iota(shape):
    return lax.broadcasted_iota(I32, shape, len(shape) - 1)


def _nt_dot(a, b):
    return lax.dot_general(a, b, (((1,), (1,)), ((), ())), preferred_element_type=F32)


def _dot(a, b):
    return jnp.dot(a, b, preferred_element_type=F32)


def _blockdiag(seg_log2):
    r = lax.broadcasted_iota(I32, (LANES, LANES), 0) >> seg_log2
    c = lax.broadcasted_iota(I32, (LANES, LANES), 1) >> seg_log2
    return jnp.where(r == c, 1.0 / (1 << seg_log2), 0.0).astype(BF16)


def _seg_mean(x2, bd):
    hi = x2.astype(BF16)
    lo = (x2 - hi.astype(F32)).astype(BF16)
    return _dot(hi, bd) + _dot(lo, bd)


def _seg_norm(x, gain, bd):
    return x * lax.rsqrt(_seg_mean(x * x, bd) + EPS) * gain


def _rope(x, cos, sin, half):
    lane = _lane_iota(x.shape)
    first = (lane & (2 * half - 1)) < half
    swapped = jnp.where(first, pltpu.roll(x, LANES - half, 1), pltpu.roll(x, half, 1))
    return x * cos + swapped * sin


def _softmax_step(s, mask, m, l):
    if mask is not None:
        s = jnp.where(mask, s, NEG)
    m_new = jnp.maximum(m, jnp.max(s, axis=1, keepdims=True))
    alpha = jnp.exp(m - m_new)
    p = jnp.exp(s - m_new)
    if mask is not None:
        p = jnp.where(mask, p, 0.0)
    l_new = alpha * l + jnp.sum(p, axis=1, keepdims=True)
    return p, alpha, m_new, l_new


def _flash_init(rows):
    return (jnp.full((rows, 1), NEG, F32), jnp.zeros((rows, 1), F32), jnp.zeros((rows, LANES), F32))


def _normalise(acc, l):
    return acc / jnp.maximum(l, 1e-30)


def _masked_softmax(s, vis, axis):
    lg = jnp.where(vis, s, NEG)
    mx = jnp.max(lg, axis=axis, keepdims=True)
    e = jnp.where(vis, jnp.exp(lg - mx), 0.0)
    return e / jnp.maximum(jnp.sum(e, axis=axis, keepdims=True), 1e-30)


def _topk_mask(imp, idx, k, axis):
    sel = jnp.zeros_like(imp)
    for _ in range(k):
        mx = jnp.max(imp, axis=axis, keepdims=True)
        first = jnp.min(jnp.where(imp == mx, idx, 1.0e9), axis=axis, keepdims=True)
        hit = idx == first
        sel = jnp.where(hit, 1.0, sel)
        imp = jnp.where(hit, -2.0, imp)
    return sel


def _proj_kernel(x_ref, g1_ref, w_ref, bias_ref, gains_ref, c64_ref, s64_ref, c32_ref, s32_ref,
                 wck_ref, wcv_ref,
                 fq_b, fk_f, fk_b, fv_f, fv_b, dq_b, dk_f, dk_b, dv_f, dv_b, nq_b,
                 ck_f, cv_f, sk_f, sv_f, wk_f, wv_f, sk_b, sv_b, wk_b, wv_b,
                 logf_o, gate_o, cmpk_o, cmpv_o):
    x = x_ref[...]
    u = x * lax.rsqrt(jnp.mean(x * x, axis=-1, keepdims=True) + EPS) * g1_ref[...]
    z = _dot(u.astype(BF16), w_ref[...])
    tm = x.shape[0]
    bd64 = _blockdiag(6)
    bd32 = _blockdiag(5)
    gains = gains_ref[...]
    c64, s64 = c64_ref[...], s64_ref[...]
    c32, s32 = c32_ref[...], s32_ref[...]
    scale = HEAD_DIM ** -0.5

    def chunk(col):
        return z[:, col:col + LANES]

    for c in range(3):
        sl = slice(c * LANES, (c + 1) * LANES)
        fq_b[:, sl] = (_seg_norm(chunk(C_FQ + c * LANES), gains[0:1], bd64) * scale).astype(BF16)
        fk = _seg_norm(chunk(C_FK + c * LANES), gains[1:2], bd64)
        fk_f[:, sl] = fk
        fk_b[:, sl] = fk.astype(BF16)
        fv = chunk(C_FV + c * LANES)
        fv_f[:, sl] = fv
        fv_b[:, sl] = fv.astype(BF16)
        nq = _rope(_seg_norm(chunk(C_NQ + c * LANES), gains[4:5], bd64), c64, s64, 32)
        nq_b[:, sl] = (nq * scale).astype(BF16)
    for c in range(2):
        sl = slice(c * LANES, (c + 1) * LANES)
        dq = _rope(_seg_norm(chunk(C_DQ + c * LANES), gains[2:3], bd32), c32, s32, 16)
        dq_b[:, sl] = dq.astype(BF16)
    dk = _rope(_seg_norm(chunk(C_DK), gains[3:4], bd32), c32, s32, 16)
    dk_f[...] = dk
    dk_b[...] = dk.astype(BF16)
    dv = chunk(C_DV)
    dv_f[...] = dv
    dv_b[...] = dv.astype(BF16)

    ck = _rope(_seg_norm(chunk(C_NKV), gains[5:6], bd64), c64, s64, 32)
    ck_f[...] = ck
    cv = chunk(C_NKV + LANES)
    cv_f[...] = cv
    nblk = tm // NSA_BLOCK
    cmpk_o[...] = jnp.sum((ck * wck_ref[...]).reshape(nblk, NSA_BLOCK, LANES), axis=1) * (1.0 / NSA_BLOCK)
    cmpv_o[...] = jnp.sum((cv * wcv_ref[...]).reshape(nblk, NSA_BLOCK, LANES), axis=1) * (1.0 / NSA_BLOCK)
    sk = _rope(_seg_norm(chunk(C_NKV + 2 * LANES), gains[6:7], bd64), c64, s64, 32)
    sk_f[...] = sk
    sk_b[...] = sk.astype(BF16)
    sv = chunk(C_NKV + 3 * LANES)
    sv_f[...] = sv
    sv_b[...] = sv.astype(BF16)
    wk = _rope(_seg_norm(chunk(C_NKV + 4 * LANES), gains[7:8], bd64), c64, s64, 32)
    wk_f[...] = wk
    wk_b[...] = wk.astype(BF16)
    wv = chunk(C_NKV + 5 * LANES)
    wv_f[...] = wv
    wv_b[...] = wv.astype(BF16)

    small = chunk(C_SMALL) + bias_ref[...]
    en = jnp.exp(-jnp.abs(small))
    logf_o[...] = jnp.minimum(small, 0.0) - jnp.log(1.0 + en)
    gate_o[...] = 1.0 / (1.0 + jnp.exp(-small))


_PROJ_OUT = (
    ("fq_b", 384, BF16), ("fk_f", 384, F32), ("fk_b", 384, BF16), ("fv_f", 384, F32), ("fv_b", 384, BF16),
    ("dq_b", 256, BF16), ("dk_f", 128, F32), ("dk_b", 128, BF16), ("dv_f", 128, F32), ("dv_b", 128, BF16),
    ("nq_b", 384, BF16),
    ("ck_f", 128, F32), ("cv_f", 128, F32), ("sk_f", 128, F32), ("sv_f", 128, F32), ("wk_f", 128, F32),
    ("wv_f", 128, F32), ("sk_b", 128, BF16), ("sv_b", 128, BF16), ("wk_b", 128, BF16), ("wv_b", 128, BF16),
    ("logf", 128, F32), ("gate", 128, F32),
)


def _project(x, lw, tabs, tm):
    n = x.shape[0]
    n_tab_blocks = tabs[0].shape[0] // tm
    nblk = tm // NSA_BLOCK
    const = lambda i: (0, 0)
    tok = lambda i: (i, 0)
    tab = lambda i: (i % n_tab_blocks, 0)
    in_specs = [
        pl.BlockSpec((tm, D_MODEL), tok),
        pl.BlockSpec((1, D_MODEL), const),
        pl.BlockSpec((D_MODEL, W_PACKED), const),
        pl.BlockSpec((1, LANES), const),
        pl.BlockSpec((8, LANES), const),
        pl.BlockSpec((tm, LANES), tab), pl.BlockSpec((tm, LANES), tab),
        pl.BlockSpec((tm, LANES), tab), pl.BlockSpec((tm, LANES), tab),
        pl.BlockSpec((tm, LANES), const), pl.BlockSpec((tm, LANES), const),
    ]
    out_shape = [jax.ShapeDtypeStruct((n, w), dt) for _, w, dt in _PROJ_OUT]
    out_specs = [pl.BlockSpec((tm, w), tok) for _, w, _ in _PROJ_OUT]
    for _ in range(2):
        out_shape.append(jax.ShapeDtypeStruct((n // tm, nblk, LANES), F32))
        out_specs.append(pl.BlockSpec((None, nblk, LANES), lambda i: (i, 0, 0)))
    outs = pl.pallas_call(
        _proj_kernel, grid=(n // tm,), in_specs=in_specs, out_specs=out_specs, out_shape=out_shape,
        compiler_params=_params(("parallel",)), name="proj",
    )(x, lw["g1"], lw["w_in"], lw["bias_small"], lw["gains"], *tabs, lw["wck"][:tm], lw["wcv"][:tm])
    res = {name: o for (name, _, _), o in zip(_PROJ_OUT, outs)}
    res["cmpk"] = outs[-2].reshape(n // NSA_BLOCK, LANES)
    res["cmpv"] = outs[-1].reshape(n // NSA_BLOCK, LANES)
    return res


def _post_kernel(x_ref, oa_ref, ob_ref, oc_ref, wo_ref, g2_ref, w1_ref, w2_ref, o_ref, *, ff_chunk):
    x = x_ref[...]
    mix = _dot(oa_ref[...].astype(BF16), wo_ref[0:384, :])
    mix = mix + _dot(ob_ref[...].astype(BF16), wo_ref[384:640, :])
    mix = mix + _dot(oc_ref[...].astype(BF16), wo_ref[640:1024, :])
    x1 = x + mix
    h = (x1 * lax.rsqrt(jnp.mean(x1 * x1, axis=-1, keepdims=True) + EPS) * g2_ref[...]).astype(BF16)
    acc = x1
    for c in range(D_FF // ff_chunk):
        a = jnp.maximum(_dot(h, w1_ref[:, c * ff_chunk:(c + 1) * ff_chunk]), 0.0)
        acc = acc + _dot((a * a).astype(BF16), w2_ref[c * ff_chunk:(c + 1) * ff_chunk, :])
    o_ref[...] = acc


def _post(x, oa, ob, oc, lw, tm):
    n = x.shape[0]
    tok = lambda i: (i, 0)
    const = lambda i: (0, 0)
    single = pl.Buffered(1)
    return pl.pallas_call(
        functools.partial(_post_kernel, ff_chunk=1024),
        grid=(n // tm,),
        in_specs=[
            pl.BlockSpec((tm, D_MODEL), tok),
            pl.BlockSpec((tm, 384), tok), pl.BlockSpec((tm, 256), tok), pl.BlockSpec((tm, 384), tok),
            pl.BlockSpec((D_MODEL, D_MODEL), const, pipeline_mode=single),
            pl.BlockSpec((1, D_MODEL), const),
            pl.BlockSpec((D_MODEL, D_FF), const, pipeline_mode=single),
            pl.BlockSpec((D_FF, D_MODEL), const, pipeline_mode=single),
        ],
        out_specs=pl.BlockSpec((tm, D_MODEL), tok),
        out_shape=jax.ShapeDtypeStruct((n, D_MODEL), F32),
        compiler_params=_params(("parallel",)), name="post",
    )(x, oa, ob, oc, lw["w_o"], lw["g2"], lw["w_ff1"], lw["w_ff2"])


def _cumsum_kernel(x_ref, o_ref, carry_ref, *, tc):
    @pl.when(pl.program_id(1) == 0)
    def _():
        carry_ref[...] = jnp.zeros_like(carry_ref)

    x = x_ref[...]
    tri = jnp.where(lax.broadcasted_iota(I32, (tc, tc), 0) <= lax.broadcasted_iota(I32, (tc, tc), 1),
                    1.0, 0.0).astype(BF16)
    hi = x.astype(BF16)
    r1 = x - hi.astype(F32)
    mid = r1.astype(BF16)
    lo = (r1 - mid.astype(F32)).astype(BF16)
    c = _dot(hi, tri) + (_dot(mid, tri) + _dot(lo, tri)) + carry_ref[:, 0:1]
    o_ref[...] = c
    carry_ref[...] = jnp.broadcast_to(c[:, tc - 1:tc], carry_ref.shape)


def _cumsum_rows(x, rb, tc=512):
    r, t = x.shape
    return pl.pallas_call(
        functools.partial(_cumsum_kernel, tc=tc),
        grid=(r // rb, t // tc),
        in_specs=[pl.BlockSpec((rb, tc), lambda i, j: (i, j))],
        out_specs=pl.BlockSpec((rb, tc), lambda i, j: (i, j)),
        out_shape=jax.ShapeDtypeStruct((r, t), F32),
        scratch_shapes=[pltpu.VMEM((rb, LANES), F32)],
        compiler_params=_params(("parallel", "arbitrary")), name="cumsum",
    )(x)


def _causal_bounds(q0, tq, tk):
    return q0 // tk, (q0 + tq - 1) // tk + 1


def _causal_flash(score_fn, update_fn, n_full, init):
    carry = lax.fori_loop(0, n_full, lambda j, c: update_fn(j, score_fn(j), c, False), init)
    return update_fn(n_full, score_fn(n_full), carry, True)


def _softmax_update(s, m, acc):
    m_new = jnp.maximum(m, jnp.max(s, axis=1, keepdims=True))
    return m_new, jnp.exp(m - m_new) * acc, jnp.exp(s - m_new).astype(BF16)


def _fox_kernel(q_ref, k_ref, v_ref, qc_ref, kc_ref, o_ref, *, tq, tk):
    q0 = pl.program_id(2) * tq
    q2 = q_ref[...]
    lo = _lane_iota((tq, LANES)) < HEAD_DIM
    lo_k = _lane_iota((tk, LANES)) < HEAD_DIM
    zero = jnp.zeros_like(q2)
    qs = (jnp.where(lo, q2, zero), jnp.where(lo, zero, q2))
    qc = qc_ref[...]
    qcs = (qc[:, 0:1], qc[:, 1:2])
    rows = q0 + lax.broadcasted_iota(I32, (tq, tk), 0)
    cols = lax.broadcasted_iota(I32, (tq, tk), 1)

    def scores_of(j):
        kb = k_ref[pl.ds(pl.multiple_of(j * tk, tk), tk), :]
        return tuple(_nt_dot(qs[e], kb) for e in range(2))

    def update(j, scores, carry, masked):
        off = pl.multiple_of(j * tk, tk)
        vb = v_ref[pl.ds(off, tk), :]
        one = jnp.ones_like(vb)
        vs = (jnp.where(lo_k, vb, one), jnp.where(lo_k, one, vb))
        parts = []
        for e in range(2):
            m, acc = carry[e]
            s = scores[e] + (qcs[e] - kc_ref[e:e + 1, pl.ds(off, tk)])
            if masked:
                s = jnp.where(rows >= off + cols, s, NEG)
            parts.append(_softmax_update(s, m, acc))
        return tuple((parts[e][0], parts[e][1] + _dot(parts[e][2], vs[e])) for e in range(2))

    init = tuple((jnp.full((tq, 1), NEG, F32), jnp.zeros((tq, LANES), F32)) for _ in range(2))
    (_, a0), (_, a1) = _causal_flash(scores_of, update, q0 // tk, init)
    o = jnp.where(lo, a0 / pltpu.roll(a0, HEAD_DIM, 1), a1 / pltpu.roll(a1, HEAD_DIM, 1))
    o_ref[...] = o.astype(BF16)


def _fox_prompt(fq, fk, fv, qc, kc, tq=512, tk=512):
    b, t, _ = fq.shape
    assert tq <= tk and tk % tq == 0
    return pl.pallas_call(
        functools.partial(_fox_kernel, tq=tq, tk=tk),
        grid=(b, 3, t // tq),
        in_specs=[
            pl.BlockSpec((None, tq, LANES), lambda i, p, q: (i, q, p)),
            pl.BlockSpec((None, t, LANES), lambda i, p, q: (i, 0, p)),
            pl.BlockSpec((None, t, LANES), lambda i, p, q: (i, 0, p)),
            pl.BlockSpec((None, None, tq, 8), lambda i, p, q: (i, p, q, 0)),
            pl.BlockSpec((None, None, 8, t), lambda i, p, q: (i, p, 0, 0)),
        ],
        out_specs=pl.BlockSpec((None, tq, LANES), lambda i, p, q: (i, q, p)),
        out_shape=jax.ShapeDtypeStruct((b, t, 384), BF16),
        compiler_params=_params(("parallel", "parallel", "arbitrary")), name="fox_prompt",
    )(fq, fk, fv, qc, kc)


def _diff_lambda(par):
    lam_init = par[4:5, 0:1]
    lam = (jnp.exp(jnp.sum(par[0:1] * par[1:2], axis=1, keepdims=True))
           - jnp.exp(jnp.sum(par[2:3] * par[3:4], axis=1, keepdims=True)) + lam_init)
    return lam, lam_init


def _diff_queries(q_ref, rows):
    lane = _lane_iota((rows, LANES))
    qs = []
    for g in range(2):
        qg = q_ref[:, g * LANES:(g + 1) * LANES].astype(F32)
        qr = pltpu.roll(qg, HEAD_DIM, 1)
        for r in range(2):
            src = qg if r == g else qr
            for c in range(2):
                lo_l = g * HEAD_DIM + c * DIFF_SUB
                msk = (lane >= lo_l) & (lane < lo_l + DIFF_SUB)
                qs.append(jnp.where(msk, src, 0.0).astype(BF16))
    return qs


def _diff_finish(outs, par, rows):
    lane = _lane_iota((rows, LANES))
    lam, lam_init = _diff_lambda(par)
    bd = _blockdiag(6)
    chunks = []
    for g in range(2):
        heads = []
        for r in range(2):
            o = outs[g * 4 + r * 2] - lam * outs[g * 4 + r * 2 + 1]
            if r != g:
                o = pltpu.roll(o, HEAD_DIM, 1)
            heads.append(o)
        chunk = jnp.where(lane < HEAD_DIM, heads[0], heads[1])
        chunks.append(_seg_norm(chunk, par[5:6], bd) * (1.0 - lam_init))
    return chunks


def _ones_outside(v, g):
    lane = _lane_iota(v.shape)
    mine = (lane >= g * HEAD_DIM) & (lane < (g + 1) * HEAD_DIM)
    return jnp.where(mine, v, jnp.ones_like(v))


def _diff_kernel(q_ref, k_ref, v_ref, par_ref, o_ref, *, tq, tk):
    q0 = pl.program_id(1) * tq
    q_all = jnp.concatenate(_diff_queries(q_ref, tq), axis=0)
    scale = DIFF_SUB ** -0.5
    half = 4 * tq
    rows = q0 + (lax.broadcasted_iota(I32, (8 * tq, tk), 0) & (tq - 1))
    cols = lax.broadcasted_iota(I32, (8 * tq, tk), 1)

    def scores_of(j):
        return _nt_dot(q_all, k_ref[pl.ds(pl.multiple_of(j * tk, tk), tk), :])

    def update(j, s, carry, masked):
        off = pl.multiple_of(j * tk, tk)
        vb = v_ref[pl.ds(off, tk), :]
        m, acc = carry
        s = s * scale
        if masked:
            s = jnp.where(rows >= off + cols, s, NEG)
        m, acc, p = _softmax_update(s, m, acc)
        pv = jnp.concatenate([_dot(p[0:half], _ones_outside(vb, 0)), _dot(p[half:], _ones_outside(vb, 1))], axis=0)
        return m, acc + pv

    init = (jnp.full((8 * tq, 1), NEG, F32), jnp.zeros((8 * tq, LANES), F32))
    _, acc = _causal_flash(scores_of, update, q0 // tk, init)
    norm = acc / pltpu.roll(acc, HEAD_DIM, 1)
    chunks = _diff_finish([norm[i * tq:(i + 1) * tq] for i in range(8)], par_ref[...], tq)
    for g in range(2):
        o_ref[:, g * LANES:(g + 1) * LANES] = chunks[g].astype(BF16)


def _diff_prompt(dq, dk, dv, par, tq=128, tk=512):
    b, t, _ = dq.shape
    assert tq <= tk and tk % tq == 0 and tq & (tq - 1) == 0
    return pl.pallas_call(
        functools.partial(_diff_kernel, tq=tq, tk=tk),
        grid=(b, t // tq),
        in_specs=[
            pl.BlockSpec((None, tq, 256), lambda i, q: (i, q, 0)),
            pl.BlockSpec((None, t, LANES), lambda i, q: (i, 0, 0)),
            pl.BlockSpec((None, t, LANES), lambda i, q: (i, 0, 0)),
            pl.BlockSpec((8, LANES), lambda i, q: (0, 0)),
        ],
        out_specs=pl.BlockSpec((None, tq, 256), lambda i, q: (i, q, 0)),
        out_shape=jax.ShapeDtypeStruct((b, t, 256), BF16),
        compiler_params=_params(("parallel", "arbitrary")), name="diff_prompt",
    )(dq, dk, dv, par)


def _nsa_queries(q_ref, rows):
    lane = _lane_iota((rows, LANES))
    qch = [q_ref[:, c * LANES:(c + 1) * LANES].astype(F32) for c in range(3)]
    out = []
    for g in range(2):
        glanes = (lane >= g * HEAD_DIM) & (lane < (g + 1) * HEAD_DIM)
        parts = []
        for i in range(3):
            h = 3 * g + i
            src = qch[h // 2]
            if (h % 2) != g:
                src = pltpu.roll(src, HEAD_DIM, 1)
            parts.append(jnp.where(glanes, src, 0.0).astype(BF16))
        out.append(jnp.concatenate(parts, axis=0))
    return out


def _tile3(x):
    return jnp.concatenate([x, x, x], axis=0)


def _nsa_combine(o_ref, gate, branches, rows):
    lane = _lane_iota((rows, LANES))
    heads = [None] * NSA_HEADS
    for g in range(2):
        oc, osel, ow = branches[g]
        for i in range(3):
            h = 3 * g + i
            sl = slice(i * rows, (i + 1) * rows)
            o = (gate[:, 6 + h:7 + h] * oc[sl] + gate[:, 12 + h:13 + h] * osel[sl]
                 + gate[:, 18 + h:19 + h] * ow[sl])
            if (h % 2) != g:
                o = pltpu.roll(o, HEAD_DIM, 1)
            heads[h] = o
    for c in range(3):
        o_ref[:, c * LANES:(c + 1) * LANES] = jnp.where(lane < HEAD_DIM, heads[2 * c], heads[2 * c + 1]).astype(BF16)


def _block_expand(off, tk):
    n = lax.broadcasted_iota(I32, (LANES, tk), 0)
    k = off + lax.broadcasted_iota(I32, (LANES, tk), 1)
    return jnp.where(n == (k >> 6), 1.0, 0.0).astype(BF16)


def _nsa_kernel(q_ref, cmpk_ref, cmpv_ref, sk_ref, sv_ref, wk_ref, wv_ref, gate_ref, o_ref, *, tq, tk, nb):
    q0 = pl.program_id(1) * tq
    qgs = _nsa_queries(q_ref, tq)
    cmpk = cmpk_ref[...].astype(BF16)
    cmpv = cmpv_ref[...].astype(BF16)
    t_qn = _tile3(q0 + lax.broadcasted_iota(I32, (tq, nb), 0))
    n_qn = _lane_iota((3 * tq, nb))
    vis = (n_qn + 1) * NSA_BLOCK - 1 <= t_qn
    n_t = lax.broadcasted_iota(I32, (nb, tq), 0)
    t_t = q0 + lax.broadcasted_iota(I32, (nb, tq), 1)
    vis_t = (n_t + 1) * NSA_BLOCK - 1 <= t_t
    cur_t = t_t >> 6
    forced_t = (n_t == 0) | (n_t == cur_t) | (n_t == cur_t - 1)
    n_tf = n_t.astype(F32)

    o_cmps, sels = [], []
    for g in range(2):
        qg = qgs[g]
        pc = _masked_softmax(_nt_dot(qg, cmpk), vis, 1)
        o_cmps.append(_dot(pc.astype(BF16), cmpv))
        sc_t = _nt_dot(cmpk, qg)
        imp = None
        for i in range(3):
            pt = _masked_softmax(sc_t[:, i * tq:(i + 1) * tq], vis_t, 0)
            imp = pt if imp is None else imp + pt
        imp = jnp.where(forced_t, FORCED_SCORE, imp)
        imp = jnp.where(n_t <= cur_t, imp, -1.0)
        sel_t = _topk_mask(imp, n_tf, NSA_TOPK, 0)
        if nb < LANES:
            sel_t = jnp.concatenate([sel_t, jnp.zeros((LANES - nb, tq), F32)], axis=0)
        sels.append(sel_t.T.astype(BF16))

    q_all = jnp.concatenate(qgs, axis=0)
    g_rows = 3 * tq
    rows = q0 + (lax.broadcasted_iota(I32, (6 * tq, tk), 0) & (tq - 1))
    cols = lax.broadcasted_iota(I32, (6 * tq, tk), 1)
    init = (jnp.full((6 * tq, 1), NEG, F32), jnp.zeros((6 * tq, LANES), F32))

    def values(v_ref, off, p):
        vb = v_ref[pl.ds(off, tk), :]
        return jnp.concatenate([_dot(p[0:g_rows], _ones_outside(vb, 0)),
                                _dot(p[g_rows:], _ones_outside(vb, 1))], axis=0)

    def sel_update(j, s, carry, masked):
        off = pl.multiple_of(j * tk, tk)
        expand = _block_expand(off, tk)
        hits = [_dot(sels[g], expand) for g in range(2)]
        vis = jnp.concatenate([hits[0]] * 3 + [hits[1]] * 3, axis=0) > 0.5
        if masked:
            vis = vis & (rows >= off + cols)
        m, acc, p = _softmax_update(jnp.where(vis, s, NEG), *carry)
        return m, acc + values(sv_ref, off, p)

    _, a_s = _causal_flash(lambda j: _nt_dot(q_all, sk_ref[pl.ds(pl.multiple_of(j * tk, tk), tk), :]),
                           sel_update, q0 // tk, init)

    def win_step(j, carry):
        off = pl.multiple_of(j * tk, tk)
        dist = rows - (off + cols)
        vis = (dist >= 0) & (dist <= WINDOW)
        s = jnp.where(vis, _nt_dot(q_all, wk_ref[pl.ds(off, tk), :]), NEG)
        m, acc = carry
        m_new = jnp.maximum(m, jnp.max(s, axis=1, keepdims=True))
        p = jnp.where(vis, jnp.exp(s - m_new), 0.0).astype(BF16)
        return m_new, jnp.exp(m - m_new) * acc + values(wv_ref, off, p)

    _, a_w = lax.fori_loop(jnp.maximum(q0 - WINDOW, 0) // tk, q0 // tk + 1, win_step, init)
    o_sel = a_s / pltpu.roll(a_s, HEAD_DIM, 1)
    o_win = a_w / pltpu.roll(a_w, HEAD_DIM, 1)
    branches = [(o_cmps[g], o_sel[g * g_rows:(g + 1) * g_rows], o_win[g * g_rows:(g + 1) * g_rows])
                for g in range(2)]
    _nsa_combine(o_ref, gate_ref[...], branches, tq)


def _nsa_prompt(nq, cmpk, cmpv, sk, sv, wk, wv, gate, tq=128, tk=512):
    b, t, _ = nq.shape
    assert tq == LANES and tk % tq == 0
    nb = t // NSA_BLOCK
    full = pl.BlockSpec((None, t, LANES), lambda i, q: (i, 0, 0))
    blk = pl.BlockSpec((None, nb, LANES), lambda i, q: (i, 0, 0))
    return pl.pallas_call(
        functools.partial(_nsa_kernel, tq=tq, tk=tk, nb=nb),
        grid=(b, t // tq),
        in_specs=[pl.BlockSpec((None, tq, 384), lambda i, q: (i, q, 0)), blk, blk, full, full, full, full,
                  pl.BlockSpec((None, tq, LANES), lambda i, q: (i, q, 0))],
        out_specs=pl.BlockSpec((None, tq, 384), lambda i, q: (i, q, 0)),
        out_shape=jax.ShapeDtypeStruct((b, t, 384), BF16),
        compiler_params=_params(("parallel", "arbitrary")), name="nsa_prompt",
    )(nq, cmpk, cmpv, sk, sv, wk, wv, gate)


ROWS = 8


def _paged_specs(block, p_per_step):
    def make(i):
        return pl.BlockSpec((None, None) + block,
                            lambda b, j, pt, lr: (lr[0], pt[b, j * p_per_step + i]) + (0,) * len(block))
    return [make(i) for i in range(p_per_step)]


def _seq_spec(block):
    return pl.BlockSpec((None,) + block, lambda b, j, pt, lr: (b,) + (0,) * len(block))


def _sample_call(kern, name, page_table, lidx, in_specs, out_specs, out_shape, scratch, args, p_per_step):
    bs, n_pages = page_table.shape
    return pl.pallas_call(
        kern,
        grid_spec=pltpu.PrefetchScalarGridSpec(
            num_scalar_prefetch=2, grid=(bs, n_pages // p_per_step),
            in_specs=in_specs, out_specs=out_specs, scratch_shapes=scratch),
        out_shape=out_shape, compiler_params=_params(("parallel", "arbitrary")), name=name,
    )(page_table, lidx, *args)


def _row_tile(n):
    t = lax.broadcasted_iota(I32, (n * ROWS, PAGE_SIZE), 0) & (ROWS - 1)
    c = lax.broadcasted_iota(I32, (n * ROWS, PAGE_SIZE), 1)
    return t, c


def _gather_logf_kernel(pt_ref, l_ref, *refs, P):
    o_ref = refs[P]
    o_ref[FOX_HEADS:8, :] = jnp.zeros((8 - FOX_HEADS, P * PAGE_SIZE), F32)
    for i in range(P):
        o_ref[0:FOX_HEADS, i * PAGE_SIZE:(i + 1) * PAGE_SIZE] = refs[i][:, 0, :]


def _gather_logf(page_table, lidx, logf_rows, P):
    bs, n_pages = page_table.shape
    past = n_pages * PAGE_SIZE
    specs = [pl.BlockSpec((None, FOX_HEADS, None, 1, PAGE_SIZE),
                          lambda b, j, pt, lr, i=i: (lr[0], 0, pt[b, j * P + i], 0, 0)) for i in range(P)]
    return _sample_call(
        functools.partial(_gather_logf_kernel, P=P), "gather_logf", page_table, lidx, specs,
        pl.BlockSpec((None, 8, P * PAGE_SIZE), lambda b, j, pt, lr: (b, 0, j)),
        jax.ShapeDtypeStruct((bs, 8, past), F32), [], (logf_rows,) * P, P)


class _Flash:
    def __init__(self, m_s, l_s, a_s, rows):
        self.m_s, self.l_s, self.a_s, self.rows = m_s, l_s, a_s, rows

    def init(self):
        self.m_s[...] = jnp.full(self.m_s.shape, NEG, F32)
        self.l_s[...] = jnp.zeros(self.l_s.shape, F32)
        self.a_s[...] = jnp.zeros(self.a_s.shape, F32)

    def update(self, idx, s, mask, pv):
        rs = slice(idx * self.rows, (idx + 1) * self.rows)
        p, alpha, m, l = _softmax_step(s, mask, self.m_s[rs, 0:1], self.l_s[rs, 0:1])
        self.a_s[rs, :] = alpha * self.a_s[rs, :] + pv(p.astype(BF16))
        self.m_s[rs, :] = jnp.broadcast_to(m, (self.rows, LANES))
        self.l_s[rs, :] = jnp.broadcast_to(l, (self.rows, LANES))

    def result(self, idx, sub=None):
        rs = slice(idx * self.rows, (idx + 1) * self.rows)
        out = _normalise(self.a_s[rs, :], self.l_s[rs, 0:1])
        return out if sub is None else out[sub]


def _flash_scratch(groups, rows):
    return [pltpu.VMEM((groups * rows, LANES), F32), pltpu.VMEM((groups * rows, LANES), F32),
            pltpu.VMEM((groups * rows, HEAD_DIM), F32)]


def _page_qk(q, k_pages):
    return jnp.concatenate([_dot(q, k.astype(BF16)) for k in k_pages], axis=1)


def _page_pv(p, v_pages):
    out = None
    for i, v in enumerate(v_pages):
        term = _nt_dot(p[:, i * PAGE_SIZE:(i + 1) * PAGE_SIZE], v.astype(BF16))
        out = term if out is None else out + term
    return out


def _fox_sample_kernel(pt_ref, l_ref, q_ref, *refs, P, ts):
    k_refs, v_refs = refs[:P], refs[P:2 * P]
    kc_ref, kcn_ref, qc_ref, kn_ref, vn_ref, o_ref, m_s, l_s, a_s = refs[2 * P:]
    j = pl.program_id(1)
    fl = _Flash(m_s, l_s, a_s, ROWS)
    pl.when(j == 0)(fl.init)
    q = q_ref[...].astype(F32)
    qc = qc_ref[...]
    qh = [q[:, h * HEAD_DIM:(h + 1) * HEAD_DIM].astype(BF16) for h in range(FOX_HEADS)]

    for h in range(FOX_HEADS):
        s = _page_qk(qh[h], [k_refs[i][h] for i in range(P)])
        s = s + (qc[:, h:h + 1] - kc_ref[h:h + 1, :])
        fl.update(h, s, None, lambda p, h=h: _page_pv(p, [v_refs[i][h] for i in range(P)]))

    @pl.when(j == pl.num_programs(1) - 1)
    def _():
        t, c = _row_tile(1)
        mask = (c <= t) & (c < ts)
        for h in range(FOX_HEADS):
            s = _page_qk(qh[h], [kn_ref[h]]) + (qc[:, h:h + 1] - kcn_ref[h:h + 1, :])
            fl.update(h, s, mask, lambda p, h=h: _page_pv(p, [vn_ref[h]]))
            o_ref[:, h * HEAD_DIM:(h + 1) * HEAD_DIM] = fl.result(h)


def _fox_sample(page_table, lidx, q, cache_k, cache_v, c_rows, qc, k_new, v_new, P, ts):
    bs, n_pages = page_table.shape
    page = (FOX_HEADS, HEAD_DIM, PAGE_SIZE)
    in_specs = ([_seq_spec((ROWS, 384))] + _paged_specs(page, P) + _paged_specs(page, P) + [
        pl.BlockSpec((None, 8, P * PAGE_SIZE), lambda b, j, pt, lr: (b, 0, j)),
        pl.BlockSpec((None, 8, PAGE_SIZE), lambda b, j, pt, lr: (b, 0, n_pages)),
        _seq_spec((ROWS, 8)), _seq_spec(page), _seq_spec(page)])
    return _sample_call(
        functools.partial(_fox_sample_kernel, P=P, ts=ts), "fox_sample", page_table, lidx, in_specs,
        _seq_spec((ROWS, 384)), jax.ShapeDtypeStruct((bs, ROWS, 384), F32), _flash_scratch(FOX_HEADS, ROWS),
        (q,) + (cache_k,) * P + (cache_v,) * P + (c_rows, c_rows, qc, k_new, v_new), P)


def _diff_sample_kernel(pt_ref, l_ref, q_ref, *refs, P, ts):
    k_refs, v_refs = refs[:P], refs[P:2 * P]
    par_ref, kn_ref, vn_ref, o_ref, m_s, l_s, a_s = refs[2 * P:]
    j = pl.program_id(1)
    fl = _Flash(m_s, l_s, a_s, 2 * ROWS)
    pl.when(j == 0)(fl.init)
    q = q_ref[...].astype(F32)
    scale = DIFF_SUB ** -0.5

    def qvar(g, c):
        cols = [(2 * g + r) * HEAD_DIM + c * DIFF_SUB for r in range(2)]
        return jnp.concatenate([q[:, lo:lo + DIFF_SUB] for lo in cols], axis=0).astype(BF16)

    qv = [[qvar(g, c) for c in range(2)] for g in range(2)]
    for g in range(2):
        for c in range(2):
            s = _page_qk(qv[g][c], [k_refs[i][g, c] for i in range(P)]) * scale
            fl.update(2 * g + c, s, None, lambda p, g=g: _page_pv(p, [v_refs[i][g] for i in range(P)]))

    @pl.when(j == pl.num_programs(1) - 1)
    def _():
        t, c_idx = _row_tile(2)
        mask = (c_idx <= t) & (c_idx < ts)
        par = par_ref[...]
        lam, lam_init = _diff_lambda(par)
        for g in range(2):
            for c in range(2):
                s = _page_qk(qv[g][c], [kn_ref[g, c]]) * scale
                fl.update(2 * g + c, s, mask, lambda p, g=g: _page_pv(p, [vn_ref[g]]))
            for r in range(2):
                sub = slice(r * ROWS, (r + 1) * ROWS)
                o = fl.result(2 * g, sub) - lam * fl.result(2 * g + 1, sub)
                y = o * lax.rsqrt(jnp.mean(o * o, axis=-1, keepdims=True) + EPS) * par[5:6, 0:HEAD_DIM]
                h = 2 * g + r
                o_ref[:, h * HEAD_DIM:(h + 1) * HEAD_DIM] = y * (1.0 - lam_init)


def _diff_sample(page_table, lidx, q, cache_k, cache_v, par, k_new, v_new, P, ts):
    bs, _ = page_table.shape
    kpage, vpage = (2, 2, DIFF_SUB, PAGE_SIZE), (2, HEAD_DIM, PAGE_SIZE)
    in_specs = ([_seq_spec((ROWS, 256))] + _paged_specs(kpage, P) + _paged_specs(vpage, P) + [
        pl.BlockSpec((8, LANES), lambda b, j, pt, lr: (0, 0)), _seq_spec(kpage), _seq_spec(vpage)])
    return _sample_call(
        functools.partial(_diff_sample_kernel, P=P, ts=ts), "diff_sample", page_table, lidx, in_specs,
        _seq_spec((ROWS, 256)), jax.ShapeDtypeStruct((bs, ROWS, 256), F32), _flash_scratch(4, 2 * ROWS),
        (q,) + (cache_k,) * P + (cache_v,) * P + (par, k_new, v_new), P)


def _nsa_group_q(q, g):
    return jnp.concatenate([q[:, (3 * g + i) * HEAD_DIM:(3 * g + i + 1) * HEAD_DIM] for i in range(3)],
                           axis=0).astype(BF16)


def _nsa_s1_kernel(pt_ref, l_ref, q_ref, *refs, P, ts, past, nbp):
    ck_refs, cv_refs = refs[:P], refs[P:2 * P]
    (w_ref, ckn_ref, cvn_ref, wkb_ref, wvb_ref, wkn_ref, wvn_ref,
     ocmp_ref, owin_ref, sel_ref, cmpk_s, cmpv_s) = refs[2 * P:]
    j = pl.program_id(1)

    @pl.when(j == 0)
    def _():
        cmpk_s[...] = jnp.zeros(cmpk_s.shape, F32)
        cmpv_s[...] = jnp.zeros(cmpv_s.shape, F32)

    def averaging(n_keys, first_block):
        blk = first_block + (lax.broadcasted_iota(I32, (n_keys, nbp), 0) >> 6)
        return jnp.where(blk == lax.broadcasted_iota(I32, (n_keys, nbp), 1), 1.0 / NSA_BLOCK, 0.0).astype(BF16)

    def block_means(pages, w, avg):
        x = jnp.concatenate(pages, axis=1) * w
        hi = x.astype(BF16)
        lo = (x - hi.astype(F32)).astype(BF16)
        return _dot(hi, avg) + _dot(lo, avg)

    avg_step = averaging(P * PAGE_SIZE, j * (2 * P))
    for g in range(2):
        cmpk_s[g] += block_means([ck_refs[i][g] for i in range(P)], w_ref[0], avg_step)
        cmpv_s[g] += block_means([cv_refs[i][g] for i in range(P)], w_ref[1], avg_step)

    @pl.when(j == pl.num_programs(1) - 1)
    def _():
        q = q_ref[...].astype(F32)
        nb_past = past // NSA_BLOCK
        nb = (past + ts + NSA_BLOCK - 1) // NSA_BLOCK
        win_buf = wkb_ref.shape[2]
        avg_new = averaging(PAGE_SIZE, nb_past)
        for g in range(2):
            cmpk_s[g] += block_means([ckn_ref[g]], w_ref[0, :, 0:PAGE_SIZE], avg_new)
            cmpv_s[g] += block_means([cvn_ref[g]], w_ref[1, :, 0:PAGE_SIZE], avg_new)
            qg = _nsa_group_q(q, g)
            sc = _dot(qg, cmpk_s[g].astype(BF16))
            n = _lane_iota((3 * ROWS, nbp))
            pos = past + (lax.broadcasted_iota(I32, (3 * ROWS, nbp), 0) & (ROWS - 1))
            pc = _masked_softmax(sc, ((n + 1) * NSA_BLOCK - 1 <= pos) & (n < nb), 1)
            ocmp_ref[g] = _nt_dot(pc.astype(BF16), cmpv_s[g].astype(BF16))
            imp = pc[0:ROWS] + pc[ROWS:2 * ROWS] + pc[2 * ROWS:3 * ROWS]
            n8 = _lane_iota((ROWS, nbp))
            cur = (past + lax.broadcasted_iota(I32, (ROWS, nbp), 0)) >> 6
            imp = jnp.where((n8 == 0) | (n8 == cur) | (n8 == cur - 1), FORCED_SCORE, imp)
            imp = jnp.where(n8 <= cur, imp, -1.0)
            sel_ref[g] = _topk_mask(imp, n8.astype(F32), NSA_TOPK, 1)
            s = jnp.concatenate([_dot(qg, wkb_ref[g].astype(BF16)), _dot(qg, wkn_ref[g].astype(BF16))], axis=1)
            width = win_buf + PAGE_SIZE
            i_idx = _lane_iota((3 * ROWS, width))
            t = lax.broadcasted_iota(I32, (3 * ROWS, width), 0) & (ROWS - 1)
            in_buf = (i_idx < win_buf) & (i_idx >= win_buf + t - WINDOW) & (i_idx >= win_buf - past)
            c_new = i_idx - win_buf
            in_new = (c_new >= 0) & (c_new <= t) & (c_new < ts)
            p = _masked_softmax(s, in_buf | in_new, 1).astype(BF16)
            owin_ref[g] = (_nt_dot(p[:, 0:win_buf], wvb_ref[g].astype(BF16))
                           + _nt_dot(p[:, win_buf:width], wvn_ref[g].astype(BF16)))


def _nsa_s1(page_table, lidx, q, cache_ck, cache_cv, cmp_w, ck_new, cv_new, win_k, win_v, wk_new, wv_new,
            P, ts, nbp):
    bs, n_pages = page_table.shape
    page = (2, HEAD_DIM, PAGE_SIZE)
    win_buf = win_k.shape[-1]
    win_spec = pl.BlockSpec((None, None, 2, HEAD_DIM, win_buf), lambda b, j, pt, lr: (lr[0], b, 0, 0, 0))
    in_specs = ([_seq_spec((ROWS, 384))] + _paged_specs(page, P) + _paged_specs(page, P) + [
        pl.BlockSpec((2, HEAD_DIM, P * PAGE_SIZE), lambda b, j, pt, lr: (0, 0, 0)),
        _seq_spec(page), _seq_spec(page), win_spec, win_spec, _seq_spec(page), _seq_spec(page)])
    o_spec = _seq_spec((2, 3 * ROWS, HEAD_DIM))
    o_shape = jax.ShapeDtypeStruct((bs, 2, 3 * ROWS, HEAD_DIM), F32)
    return _sample_call(
        functools.partial(_nsa_s1_kernel, P=P, ts=ts, past=n_pages * PAGE_SIZE, nbp=nbp), "nsa_sample_cmp",
        page_table, lidx, in_specs, [o_spec, o_spec, _seq_spec((2, ROWS, nbp))],
        [o_shape, o_shape, jax.ShapeDtypeStruct((bs, 2, ROWS, nbp), F32)],
        [pltpu.VMEM((2, HEAD_DIM, nbp), F32), pltpu.VMEM((2, HEAD_DIM, nbp), F32)],
        (q,) + (cache_ck,) * P + (cache_cv,) * P + (cmp_w, ck_new, cv_new, win_k, win_v, wk_new, wv_new), P)


def _nsa_s2_kernel(pt_ref, l_ref, q_ref, *refs, P, ts, past, nbp):
    sk_refs, sv_refs = refs[:P], refs[P:2 * P]
    sel_ref, skn_ref, svn_ref, ocmp_ref, owin_ref, gate_ref, o_ref, m_s, l_s, a_s = refs[2 * P:]
    j = pl.program_id(1)
    fl = _Flash(m_s, l_s, a_s, 3 * ROWS)
    pl.when(j == 0)(fl.init)
    q = q_ref[...].astype(F32)
    tk = P * PAGE_SIZE
    n = lax.broadcasted_iota(I32, (nbp, tk), 0)
    k = j * tk + lax.broadcasted_iota(I32, (nbp, tk), 1)
    expand = jnp.where(n == (k >> 6), 1.0, 0.0).astype(BF16)
    qgs = [_nsa_group_q(q, g) for g in range(2)]
    for g in range(2):
        s = _page_qk(qgs[g], [sk_refs[i][g] for i in range(P)])
        hit = _dot(sel_ref[g].astype(BF16), expand)
        fl.update(g, s, _tile3(hit) > 0.5, lambda p, g=g: _page_pv(p, [sv_refs[i][g] for i in range(P)]))

    @pl.when(j == pl.num_programs(1) - 1)
    def _():
        t, c = _row_tile(1)
        nb_past = past // NSA_BLOCK
        gate = gate_ref[...]
        for g in range(2):
            chosen = sel_ref[g][:, nb_past:nb_past + 1] > 0.5
            hit = jnp.where(chosen & (c <= t) & (c < ts), 1.0, 0.0)
            s = _page_qk(qgs[g], [skn_ref[g]])
            fl.update(g, s, _tile3(hit) > 0.5, lambda p, g=g: _page_pv(p, [svn_ref[g]]))
            for i in range(3):
                h = 3 * g + i
                sub = slice(i * ROWS, (i + 1) * ROWS)
                o = (gate[:, 6 + h:7 + h] * ocmp_ref[g, sub, :] + gate[:, 12 + h:13 + h] * fl.result(g, sub)
                     + gate[:, 18 + h:19 + h] * owin_ref[g, sub, :])
                o_ref[:, h * HEAD_DIM:(h + 1) * HEAD_DIM] = o


def _nsa_s2(page_table, lidx, q, cache_sk, cache_sv, sel, sk_new, sv_new, ocmp, owin, gate, P, ts, nbp):
    bs, n_pages = page_table.shape
    page = (2, HEAD_DIM, PAGE_SIZE)
    o_spec = _seq_spec((2, 3 * ROWS, HEAD_DIM))
    in_specs = ([_seq_spec((ROWS, 384))] + _paged_specs(page, P) + _paged_specs(page, P) + [
        _seq_spec((2, ROWS, nbp)), _seq_spec(page), _seq_spec(page), o_spec, o_spec, _seq_spec((ROWS, LANES))])
    return _sample_call(
        functools.partial(_nsa_s2_kernel, P=P, ts=ts, past=n_pages * PAGE_SIZE, nbp=nbp), "nsa_sample_sel",
        page_table, lidx, in_specs, _seq_spec((ROWS, 384)), jax.ShapeDtypeStruct((bs, ROWS, 384), F32),
        _flash_scratch(2, 3 * ROWS),
        (q,) + (cache_sk,) * P + (cache_sv,) * P + (sel, sk_new, sv_new, ocmp, owin, gate), P)


def _rope_tables(pos, dim):
    inv = jnp.exp(-math.log(ROPE_THETA) * jnp.arange(0, dim, 2, dtype=F32) / dim)
    ang = pos.astype(F32)[:, None] * inv[None, :]
    cos, sin = jnp.cos(ang), jnp.sin(ang)
    reps = LANES // dim
    return (jnp.tile(jnp.concatenate([cos, cos], axis=1), (1, reps)),
            jnp.tile(jnp.concatenate([-sin, sin], axis=1), (1, reps)))


def _layer_weights(l, w_in, b_fox_f, fox_qn, fox_kn, diff_qn, diff_kn, diff_lambda, diff_subln, nsa_qn,
                   nsa_kn, nsa_cmp_w, nsa_gate_b, w_o, norm1, norm2, w_ff1, w_ff2, tm_max):
    fq, fk, fv, ff, dq, dk, dv, nq, nkv, ng = jnp.split(w_in[l], SPLIT_POINTS, axis=1)
    small = jnp.concatenate([ff, ng, jnp.zeros((D_MODEL, LANES - 24), F32)], axis=1)
    w_packed = jnp.concatenate([fq, fk, fv, dq, dk, dv, nq, nkv, small], axis=1).astype(BF16)
    t2 = lambda v: jnp.tile(v, LANES // v.shape[0])
    gains = jnp.stack([t2(fox_qn[l]), t2(fox_kn[l]), t2(diff_qn[l]), t2(diff_kn[l]), t2(nsa_qn[l]),
                       t2(nsa_kn[l, 0]), t2(nsa_kn[l, 1]), t2(nsa_kn[l, 2])], axis=0)
    bias_small = jnp.concatenate([b_fox_f[l], nsa_gate_b[l], jnp.zeros((LANES - 24,), F32)])[None, :]
    lam_init = 0.8 - 0.6 * math.exp(-0.3 * l)
    dpar = jnp.zeros((8, LANES), F32)
    dpar = dpar.at[0:4, 0:DIFF_SUB].set(diff_lambda[l].astype(F32))
    dpar = dpar.at[4, :].set(lam_init)
    dpar = dpar.at[5, :].set(t2(diff_subln[l]))
    reps = tm_max // NSA_BLOCK
    return dict(
        w_in=w_packed, gains=gains, bias_small=bias_small, dpar=dpar,
        g1=norm1[l][None, :], g2=norm2[l][None, :],
        wck=jnp.tile(jnp.tile(nsa_cmp_w[l, 0], (1, 2)), (reps, 1)),
        wcv=jnp.tile(jnp.tile(nsa_cmp_w[l, 1], (1, 2)), (reps, 1)),
        cmp_w=jnp.tile(jnp.swapaxes(nsa_cmp_w[l], 1, 2), (1, 1, 2 * PAGES_PER_STEP)),
        w_o=w_o[l].astype(BF16), w_ff1=w_ff1[l].astype(BF16), w_ff2=w_ff2[l].astype(BF16),
    )


def _prompt_layer(x, lw, tabs, b, t, tm):
    pr = _project(x, lw, tabs, tm)
    r3 = lambda a: a.reshape(b, t, a.shape[-1])
    logf_rows = jnp.swapaxes(r3(pr["logf"])[:, :, 0:8], 1, 2)
    c_rows = _cumsum_rows(logf_rows.reshape(b * 8, t), b * 8).reshape(b, 8, t)[:, 0:FOX_HEADS]
    kc = jnp.pad(c_rows.reshape(b, 3, 2, t), ((0, 0), (0, 0), (0, 6), (0, 0)))
    qc = jnp.swapaxes(kc, 2, 3)
    o_a = _fox_prompt(r3(pr["fq_b"]), r3(pr["fk_b"]), r3(pr["fv_b"]), qc, kc)
    o_b = _diff_prompt(r3(pr["dq_b"]), r3(pr["dk_b"]), r3(pr["dv_b"]), lw["dpar"])
    nb = t // NSA_BLOCK
    o_c = _nsa_prompt(r3(pr["nq_b"]), pr["cmpk"].reshape(b, nb, LANES), pr["cmpv"].reshape(b, nb, LANES),
                      r3(pr["sk_b"]), r3(pr["sv_b"]), r3(pr["wk_b"]), r3(pr["wv_b"]), r3(pr["gate"]))
    x_out = _post(x, o_a.reshape(b * t, 384), o_b.reshape(b * t, 256), o_c.reshape(b * t, 384), lw, tm)
    return x_out, pr


PAGES_PER_STEP = 8
CUMSUM_TILE = 512


def _sample_layer(x, lw, tabs, caches, page_table, l, bs, ts, p_step=PAGES_PER_STEP):
    (c_fk, c_fv, c_fl, c_dk, c_dv, c_ck, c_cv, c_sk, c_sv, s_wk, s_wv) = caches
    n_pages = page_table.shape[1]
    past = n_pages * PAGE_SIZE
    lidx = jnp.full((1,), l, I32)
    pr = _project(x, lw, tabs, bs * ts)
    seq = lambda a, tail: a.reshape((bs, ts) + tail)
    pad_q = lambda a: jnp.pad(seq(a, a.shape[1:]), ((0, 0), (0, ROWS - ts), (0, 0)))
    def new_page(a, tail):
        a = jnp.moveaxis(seq(a, tail), 1, -1)
        return jnp.pad(a, ((0, 0),) * (a.ndim - 1) + ((0, PAGE_SIZE - ts),))

    lf_past = _gather_logf(page_table, lidx, c_fl, p_step)
    lf_new = jnp.swapaxes(seq(pr["logf"], (LANES,))[:, :, 0:8], 1, 2)
    rows_all = jnp.concatenate([lf_past, jnp.pad(lf_new, ((0, 0), (0, 0), (0, CUMSUM_TILE - ts)))], axis=2)
    width = past + CUMSUM_TILE
    c_rows = _cumsum_rows(rows_all.reshape(bs * 8, width), min(bs * 8, 128), CUMSUM_TILE).reshape(bs, 8, width)
    qc = jnp.swapaxes(c_rows[:, :, past:past + ROWS], 1, 2)

    o_a = _fox_sample(page_table, lidx, pad_q(pr["fq_b"]), c_fk, c_fv, c_rows, qc,
                      new_page(pr["fk_f"], (FOX_HEADS, HEAD_DIM)), new_page(pr["fv_f"], (FOX_HEADS, HEAD_DIM)),
                      p_step, ts)
    o_b = _diff_sample(page_table, lidx, pad_q(pr["dq_b"]), c_dk, c_dv, lw["dpar"],
                       new_page(pr["dk_f"], (2, 2, DIFF_SUB)), new_page(pr["dv_f"], (2, HEAD_DIM)), p_step, ts)
    nq = pad_q(pr["nq_b"])
    kv = lambda name: new_page(pr[name], (2, HEAD_DIM))
    nbp = -(-(past // NSA_BLOCK + 2) // LANES) * LANES
    ocmp, owin, sel = _nsa_s1(page_table, lidx, nq, c_ck, c_cv, lw["cmp_w"], kv("ck_f"), kv("cv_f"),
                              s_wk, s_wv, kv("wk_f"), kv("wv_f"), p_step, ts, nbp)
    o_c = _nsa_s2(page_table, lidx, nq, c_sk, c_sv, sel, kv("sk_f"), kv("sv_f"), ocmp, owin,
                  pad_q(pr["gate"]), p_step, ts, nbp)
    unpad = lambda o: o[:, 0:ts].reshape(bs * ts, o.shape[-1])
    x_out = _post(x, unpad(o_a), unpad(o_b), unpad(o_c), lw, bs * ts)
    return x_out, pr


def kernel(x_prompt, x_sample, cache_fox_k, cache_fox_v, cache_fox_logf, cache_diff_k, cache_diff_v,
           cache_nsa_cmp_k, cache_nsa_cmp_v, cache_nsa_sel_k, cache_nsa_sel_v, state_nsa_win_k,
           state_nsa_win_v, page_table, w_in, b_fox_f, fox_qn, fox_kn, diff_qn, diff_kn, diff_lambda,
           diff_subln, nsa_qn, nsa_kn, nsa_cmp_w, nsa_gate_b, w_o, norm1, norm2, w_ff1, w_ff2):
    b, t, _ = x_prompt.shape
    bs, ts, _ = x_sample.shape
    depth = w_in.shape[0]
    past = page_table.shape[1] * PAGE_SIZE
    win_buf = state_nsa_win_k.shape[2]
    tm = 256
    key_minor = lambda c: jnp.moveaxis(c, 2, -1)
    logf_rows = jnp.transpose(cache_fox_logf, (0, 3, 1, 2))[:, :, :, None, :]
    sample_caches = (key_minor(cache_fox_k), key_minor(cache_fox_v), logf_rows, key_minor(cache_diff_k),
                     key_minor(cache_diff_v), key_minor(cache_nsa_cmp_k), key_minor(cache_nsa_cmp_v),
                     key_minor(cache_nsa_sel_k), key_minor(cache_nsa_sel_v), key_minor(state_nsa_win_k),
                     key_minor(state_nsa_win_v))
    pos = jnp.arange(t, dtype=I32)
    tabs_p = _rope_tables(pos, HEAD_DIM) + _rope_tables(pos, DIFF_SUB)
    pos_s = jnp.tile(past + jnp.arange(ts, dtype=I32), bs)
    tabs_s = _rope_tables(pos_s, HEAD_DIM) + _rope_tables(pos_s, DIFF_SUB)
    xp = x_prompt.reshape(b * t, D_MODEL)
    xs = x_sample.reshape(bs * ts, D_MODEL)
    new_p = [[] for _ in range(11)]
    new_s = [[] for _ in range(11)]
    for l in range(depth):
        lw = _layer_weights(l, w_in, b_fox_f, fox_qn, fox_kn, diff_qn, diff_kn, diff_lambda, diff_subln,
                            nsa_qn, nsa_kn, nsa_cmp_w, nsa_gate_b, w_o, norm1, norm2, w_ff1, w_ff2, tm)
        xp, pr = _prompt_layer(xp, lw, tabs_p, b, t, tm)
        xs, ps = _sample_layer(xs, lw, tabs_s, sample_caches, page_table, l, bs, ts)
        for grp, rows, lead in ((new_p, pr, (b, t)), (new_s, ps, (bs, ts))):
            shp = lambda a, tail: a.reshape(lead + tail)
            grp[0].append(shp(rows["fk_f"], (FOX_HEADS, HEAD_DIM)))
            grp[1].append(shp(rows["fv_f"], (FOX_HEADS, HEAD_DIM)))
            grp[2].append(shp(rows["logf"], (LANES,))[..., 0:FOX_HEADS])
            grp[3].append(shp(rows["dk_f"], (2, 2, DIFF_SUB)))
            grp[4].append(shp(rows["dv_f"], (2, HEAD_DIM)))
            for n, name in enumerate(("ck_f", "cv_f", "sk_f", "sv_f", "wk_f", "wv_f")):
                grp[5 + n].append(shp(rows[name], (2, HEAD_DIM)))
        for n, state in ((9, state_nsa_win_k), (10, state_nsa_win_v)):
            new_p[n][-1] = jnp.pad(new_p[n][-1], ((0, 0), (win_buf, 0), (0, 0), (0, 0)))[:, -win_buf:]
            new_s[n][-1] = jnp.concatenate([state[l], new_s[n][-1]], axis=1)[:, -win_buf:]
    outs = [xp.reshape(b, t, D_MODEL), xs.reshape(bs, ts, D_MODEL)]
    for n in range(11):
        outs.append(jnp.stack(new_p[n], axis=0))
        outs.append(jnp.stack(new_s[n], axis=0))
    return tuple(outs)
```
